```python
import jax, jax.numpy as jnp
from jax import lax
import numpy as np

D_MODEL = 1024
BATCH = 4
SEQ = 8192
DEPTH = 1

CTX_LEN = 256
GRID_W = 64

D_SSD = 1024
SSD_HEAD_DIM = 64
SSD_HEADS = D_SSD // SSD_HEAD_DIM
SSD_GROUPS = 4
SSD_HPG = SSD_HEADS // SSD_GROUPS
SSD_STATE = 128
SSD_CONV = 5
SSD_CHUNK = 128
D_CONV = 1024
SC_WIDTH = 3
D_MIX = D_SSD + D_CONV
XBC_DIM = D_SSD + 2 * SSD_GROUPS * SSD_STATE
IN_COLS = D_SSD + XBC_DIM + 2 * SSD_HEADS + 3 * D_CONV

PEER_HEADS = 8
PEER_N_KEYS = 128
PEER_N_EXPERTS = PEER_N_KEYS * PEER_N_KEYS
PEER_TOPK = 16
PEER_QUERY_DIM = 256
PEER_HALF = PEER_QUERY_DIM // 2
PEER_BLOCK = 128

N_MOD = 6
EPS = 1e-6

kernel_name = "hymba_ssd_shortconv_peer_prefix_block"


def rmsnorm(x, g):
    xf = x.astype(jnp.float32)
    y = xf * lax.rsqrt(jnp.mean(xf * xf, axis=-1, keepdims=True) + EPS)
    return y.astype(x.dtype) * g


def modulate(x, shift, scale):
    return x * (1 + scale) + shift


def dwconv_centred(u, w):
    K = w.shape[0]
    p = K // 2
    L = u.shape[-2]
    up = jnp.pad(u, [(0, 0)] * (u.ndim - 2) + [(p, p), (0, 0)])
    out = up[..., 0:L, :] * w[0]
    for k in range(1, K):
        out = out + up[..., k:k + L, :] * w[k]
    return out


def conv_rows(u, w):
    b, L, C = u.shape
    rows = L // GRID_W
    return dwconv_centred(u.reshape(b, rows, GRID_W, C), w).reshape(b, L, C)


def split_proj(proj):
    o = 0
    z = proj[..., o:o + D_SSD]; o += D_SSD
    xbc = proj[..., o:o + XBC_DIM]; o += XBC_DIM
    dt = proj[..., o:o + 2 * SSD_HEADS]; o += 2 * SSD_HEADS
    g_b = proj[..., o:o + D_CONV]; o += D_CONV
    g_c = proj[..., o:o + D_CONV]; o += D_CONV
    v = proj[..., o:o + D_CONV]
    return z, xbc, dt, g_b, g_c, v


def ssd_chunk_scan(xdt, a, bm, cm, h0):
    b, L, G, R, P = xdt.shape
    N = bm.shape[-1]
    Q = SSD_CHUNK
    nc = L // Q
    xdt = xdt.reshape(b, nc, Q, G, R, P)
    a = a.reshape(b, nc, Q, G, R)
    bm = bm.reshape(b, nc, Q, G, N)
    cm = cm.reshape(b, nc, Q, G, N)
    acs = jnp.cumsum(a, axis=2)
    lower = jnp.tril(jnp.ones((Q, Q), bool))[:, :, None, None]
    seg = acs[:, :, :, None] - acs[:, :, None, :]
    decay = jnp.exp(jnp.where(lower, seg, -jnp.inf))
    cb = jnp.einsum('bclgn,bcsgn->bclsg', cm, bm)
    y_diag = jnp.einsum('bclsg,bclsgr,bcsgrp->bclgrp', cb, decay, xdt)
    to_end = jnp.exp(acs[:, :, -1:] - acs)
    states = jnp.einsum('bclgn,bclgr,bclgrp->bcgrpn', bm, to_end, xdt)
    chunk_decay = jnp.exp(acs[:, :, -1])

    def step(h, inp):
        st, dec = inp
        return h * dec[..., None, None] + st, h

    h_final, h_enter = lax.scan(step, h0, (jnp.moveaxis(states, 1, 0), jnp.moveaxis(chunk_decay, 1, 0)))
    h_enter = jnp.moveaxis(h_enter, 0, 1)
    y_off = jnp.einsum('bclgn,bcgrpn,bclgr->bclgrp', cm, h_enter, jnp.exp(acs))
    return (y_diag + y_off).reshape(b, L, G, R, P), h_final


def ssd_prepare(xbc, dt_raw, conv_fn, conv_w, conv_b, dt_bias, a_log):
    xbc = jax.nn.silu(conv_fn(xbc, conv_w) + conv_b).astype(jnp.float32)
    b, L, _ = xbc.shape
    gn = SSD_GROUPS * SSD_STATE
    xs = xbc[..., :D_SSD].reshape(b, L, SSD_GROUPS, SSD_HPG, SSD_HEAD_DIM)
    bm = xbc[..., D_SSD:D_SSD + gn].reshape(b, L, SSD_GROUPS, SSD_STATE)
    cm = xbc[..., D_SSD + gn:].reshape(b, L, SSD_GROUPS, SSD_STATE)
    dt = jax.nn.softplus(dt_raw.astype(jnp.float32).reshape(b, L, 2, SSD_HEADS) + dt_bias.astype(jnp.float32))
    dt = dt.reshape(b, L, 2, SSD_GROUPS, SSD_HPG)
    a = dt * (-jnp.exp(a_log.astype(jnp.float32))).reshape(2, SSD_GROUPS, SSD_HPG)
    return xs, bm, cm, dt, a


def ssd_direction(xs, bm, cm, dt, a, h0, reverse):
    xdt = xs * dt[..., None]
    if reverse:
        xdt, a, bm, cm = jnp.flip(xdt, 1), jnp.flip(a, 1), jnp.flip(bm, 1), jnp.flip(cm, 1)
    y, h = ssd_chunk_scan(xdt, a, bm, cm, h0)
    if reverse:
        y = jnp.flip(y, 1)
    return y, h


def ssd_bidirectional(xs, bm, cm, dt, a, h0_fwd, h0_bwd):
    y_f, h_f = ssd_direction(xs, bm, cm, dt[:, :, 0], a[:, :, 0], h0_fwd, False)
    y_b, h_b = ssd_direction(xs, bm, cm, dt[:, :, 1], a[:, :, 1], h0_bwd, True)
    return y_f + y_b, h_f, h_b


def ssd_output(y, xs, z, d_skip, norm_g):
    b, L = y.shape[:2]
    y = y + d_skip.astype(jnp.float32).reshape(SSD_GROUPS, SSD_HPG)[..., None] * xs
    y = y.reshape(b, L, D_SSD) * jax.nn.silu(z.astype(jnp.float32))
    yg = y.reshape(b, L, SSD_GROUPS, D_SSD // SSD_GROUPS)
    yg = yg * lax.rsqrt(jnp.mean(yg * yg, axis=-1, keepdims=True) + EPS)
    return yg.reshape(b, L, D_SSD).astype(z.dtype) * norm_g


def shortconv_branch(g_b, g_c, v, conv_fn, w, norm_g):
    return rmsnorm(g_b * conv_fn(g_c * v, w), norm_g)


def token_mixer(h_lat, h_ctx, w_in, ssd_conv_w, ssd_conv_b, dt_bias, a_log, d_skip, ssd_norm_g,
                sc_conv_w, sc_norm_g, w_out, with_ctx_out):
    b = h_lat.shape[0]
    z_l, xbc_l, dt_l, gb_l, gc_l, v_l = split_proj(h_lat @ w_in)
    z_c, xbc_c, dt_c, gb_c, gc_c, v_c = split_proj(h_ctx @ w_in)
    xs_c, bm_c, cm_c, dtd_c, a_c = ssd_prepare(xbc_c, dt_c, dwconv_centred, ssd_conv_w, ssd_conv_b, dt_bias, a_log)
    h0 = jnp.zeros((b, SSD_GROUPS, SSD_HPG, SSD_HEAD_DIM, SSD_STATE), jnp.float32)
    y_c, hc_f, hc_b = ssd_bidirectional(xs_c, bm_c, cm_c, dtd_c, a_c, h0, h0)
    xs_l, bm_l, cm_l, dtd_l, a_l = ssd_prepare(xbc_l, dt_l, conv_rows, ssd_conv_w, ssd_conv_b, dt_bias, a_log)
    y_l, _, _ = ssd_bidirectional(xs_l, bm_l, cm_l, dtd_l, a_l, hc_f, hc_b)
    out_l = jnp.concatenate([ssd_output(y_l, xs_l, z_l, d_skip, ssd_norm_g),
                             shortconv_branch(gb_l, gc_l, v_l, conv_rows, sc_conv_w, sc_norm_g)], axis=-1) @ w_out
    out_c = None
    if with_ctx_out:
        out_c = jnp.concatenate([ssd_output(y_c, xs_c, z_c, d_skip, ssd_norm_g),
                                 shortconv_branch(gb_c, gc_c, v_c, dwconv_centred, sc_conv_w, sc_norm_g)], axis=-1) @ w_out
    return out_l, out_c


def peer(h, w_q, keys, u_tab, v_tab):
    b, L, D = h.shape
    tok = h.reshape(-1, PEER_BLOCK, D)

    def block(xb):
        T = xb.shape[0]
        q = (xb @ w_q).reshape(T, PEER_HEADS, 2, PEER_HALF)
        s = jnp.einsum('thpd,hpnd->thpn', q, keys).astype(jnp.float32)
        sv, si = lax.top_k(s, PEER_TOPK)
        cand = (sv[:, :, 0, :, None] + sv[:, :, 1, None, :]).reshape(T, PEER_HEADS, PEER_TOPK * PEER_TOPK)
        cv, ci = lax.top_k(cand, PEER_TOPK)
        i1 = jnp.take_along_axis(si[:, :, 0], ci // PEER_TOPK, axis=-1)
        i2 = jnp.take_along_axis(si[:, :, 1], ci % PEER_TOPK, axis=-1)
        eid = i1 * PEER_N_KEYS + i2
        g = jax.nn.softmax(cv, axis=-1).astype(xb.dtype)
        u = jnp.take(u_tab, eid, axis=0)
        act = jax.nn.gelu(jnp.einsum('thkd,td->thk', u, xb))
        return jnp.einsum('thk,thkd->td', g * act, jnp.take(v_tab, eid, axis=0))

    return lax.map(block, tok).reshape(b, L, D)


def setup_inputs(seed: int = 0) -> dict:
    key = jax.random.key(seed)
    ks = jax.random.split(key, 24)
    nrm = lambda k, shape, s: jax.random.normal(k, shape, jnp.float32) * s
    gain = lambda k, n: 1.0 + nrm(k, (DEPTH, n), 0.05)
    u = jax.random.uniform(ks[10], (DEPTH, 2, SSD_HEADS), jnp.float32)
    dt0 = jnp.exp(u * (np.log(0.1) - np.log(0.001)) + np.log(0.001)).astype(jnp.float32)
    dt_bias = dt0 + jnp.log(-jnp.expm1(-dt0))
    a_log = jnp.log(jax.random.uniform(ks[11], (DEPTH, 2, SSD_HEADS), jnp.float32, 1.0, 16.0))
    return {
        'x': nrm(ks[0], (BATCH, SEQ, D_MODEL), 1.0),
        'c': nrm(ks[1], (BATCH, D_MODEL), 1.0),
        'ctx': nrm(ks[2], (BATCH, CTX_LEN, D_MODEL), 1.0),
        'c_ctx': nrm(ks[3], (D_MODEL,), 1.0),
        'w_ada': nrm(ks[4], (DEPTH, D_MODEL, N_MOD * D_MODEL), 0.5 * D_MODEL ** -0.5),
        'b_ada': nrm(ks[5], (DEPTH, N_MOD * D_MODEL), 0.02),
        'g_pre_mix': gain(ks[6], D_MODEL),
        'g_post_mix': gain(ks[7], D_MODEL),
        'g_pre_ffn': gain(ks[8], D_MODEL),
        'g_post_ffn': gain(ks[9], D_MODEL),
        'w_in': nrm(ks[12], (DEPTH, D_MODEL, IN_COLS), D_MODEL ** -0.5),
        'ssd_conv_w': nrm(ks[13], (DEPTH, SSD_CONV, XBC_DIM), SSD_CONV ** -0.5),
        'ssd_conv_b': nrm(ks[14], (DEPTH, XBC_DIM), 0.02),
        'ssd_dt_bias': dt_bias,
        'ssd_a_log': a_log,
        'ssd_d': 1.0 + nrm(ks[15], (DEPTH, SSD_HEADS), 0.1),
        'ssd_norm_g': gain(ks[16], D_SSD),
        'sc_conv_w': nrm(ks[17], (DEPTH, SC_WIDTH, D_CONV), SC_WIDTH ** -0.5),
        'sc_norm_g': gain(ks[18], D_CONV),
        'w_out': nrm(ks[19], (DEPTH, D_MIX, D_MODEL), D_MIX ** -0.5),
        'peer_w_q': nrm(ks[20], (DEPTH, D_MODEL, PEER_HEADS * PEER_QUERY_DIM), D_MODEL ** -0.5),
        'peer_keys': nrm(ks[21], (DEPTH, PEER_HEADS, 2, PEER_N_KEYS, PEER_HALF), PEER_HALF ** -0.5),
        'peer_u': nrm(ks[22], (DEPTH, PEER_N_EXPERTS, D_MODEL), D_MODEL ** -0.5),
        'peer_v': nrm(ks[23], (DEPTH, PEER_N_EXPERTS, D_MODEL), D_MODEL ** -0.5),
    }


def reference(x, c, ctx, c_ctx, w_ada, b_ada, g_pre_mix, g_post_mix, g_pre_ffn, g_post_ffn,
              w_in, ssd_conv_w, ssd_conv_b, ssd_dt_bias, ssd_a_log, ssd_d, ssd_norm_g,
              sc_conv_w, sc_norm_g, w_out, peer_w_q, peer_keys, peer_u, peer_v):
    x_lat, x_ctx = x, ctx
    for layer in range(DEPTH):
        last = layer == DEPTH - 1
        mod_l = (jax.nn.silu(c) @ w_ada[layer] + b_ada[layer])[:, None, :]
        mod_c = jax.nn.silu(c_ctx) @ w_ada[layer] + b_ada[layer]
        sh1_l, sc1_l, gt1_l, sh2_l, sc2_l, gt2_l = jnp.split(mod_l, N_MOD, axis=-1)
        sh1_c, sc1_c, gt1_c, sh2_c, sc2_c, gt2_c = jnp.split(mod_c, N_MOD, axis=-1)
        h_l = modulate(rmsnorm(x_lat, g_pre_mix[layer]), sh1_l, sc1_l)
        h_c = modulate(rmsnorm(x_ctx, g_pre_mix[layer]), sh1_c, sc1_c)
        mix_l, mix_c = token_mixer(h_l, h_c, w_in[layer], ssd_conv_w[layer], ssd_conv_b[layer],
                                   ssd_dt_bias[layer], ssd_a_log[layer], ssd_d[layer], ssd_norm_g[layer],
                                   sc_conv_w[layer], sc_norm_g[layer], w_out[layer], not last)
        x_lat = x_lat + gt1_l * rmsnorm(mix_l, g_post_mix[layer])
        h2_l = modulate(rmsnorm(x_lat, g_pre_ffn[layer]), sh2_l, sc2_l)
        x_lat = x_lat + gt2_l * rmsnorm(peer(h2_l, peer_w_q[layer], peer_keys[layer], peer_u[layer], peer_v[layer]),
                                        g_post_ffn[layer])
        if not last:
            x_ctx = x_ctx + gt1_c * rmsnorm(mix_c, g_post_mix[layer])
            h2_c = modulate(rmsnorm(x_ctx, g_pre_ffn[layer]), sh2_c, sc2_c)
            x_ctx = x_ctx + gt2_c * rmsnorm(peer(h2_c, peer_w_q[layer], peer_keys[layer], peer_u[layer], peer_v[layer]),
                                            g_post_ffn[layer])
    return x_lat
```

```python
import functools

import jax
import jax.numpy as jnp
from jax import lax
from jax.experimental import pallas as pl
from jax.experimental.pallas import tpu as pltpu

GRID_W = 64
SSD_HEAD_DIM = 64
SSD_GROUPS = 4
SSD_STATE = 128
SSD_CHUNK = 128
PEER_HEADS = 8
PEER_N_KEYS = 128
PEER_TOPK = 16
EPS = 1e-6

MXU_DTYPE = jnp.bfloat16
VMEM_LIMIT_BYTES = 56 * 1024 * 1024

PROJ_TOKENS = 256
ROUTE_TOKENS = 256
PEER_TOK_BLOCK = 128
PEER_RING = 4

F32 = jnp.float32


def _dot(a, b):
    return jnp.dot(a, b, preferred_element_type=F32)


def _dot_nt(a, b):
    return lax.dot_general(a, b, (((1,), (1,)), ((), ())), preferred_element_type=F32)


def _dot_tn(a, b):
    return lax.dot_general(a, b, (((0,), (0,)), ((), ())), preferred_element_type=F32)


def _split_bf16(a, terms):
    parts = []
    r = a
    for i in range(terms):
        p = r.astype(MXU_DTYPE)
        parts.append(p)
        if i + 1 < terms:
            r = r - p.astype(F32)
    return parts


def _silu(x):
    return x * jax.nn.sigmoid(x)


def _softplus(x):
    return jnp.maximum(x, 0.0) + jnp.log1p(jnp.exp(-jnp.abs(x)))


def _gelu_tanh(x):
    return 0.5 * x * (1.0 + jnp.tanh(0.7978845608028654 * (x + 0.044715 * (x * x * x))))


def _rms(x):
    return x * lax.rsqrt(jnp.mean(x * x, axis=-1, keepdims=True) + EPS)


def _const_spec(shape):
    zeros = (0,) * len(shape)
    return pl.BlockSpec(shape, lambda *_: zeros, pipeline_mode=pl.Buffered(1))


def _adaln_kernel(c_ref, w_ref, b_ref, o_ref):
    s = _silu(c_ref[...]).astype(MXU_DTYPE)
    o_ref[...] = _dot(s, w_ref[...].astype(MXU_DTYPE)) + b_ref[...]


def _adaln(cc, w, b):
    rows, d = cc.shape
    n = w.shape[1]
    bn = 1024
    return pl.pallas_call(
        _adaln_kernel,
        grid=(n // bn,),
        in_specs=[pl.BlockSpec((rows, d), lambda j: (0, 0)),
                  pl.BlockSpec((d, bn), lambda j: (0, j)),
                  pl.BlockSpec((1, bn), lambda j: (0, j))],
        out_specs=pl.BlockSpec((rows, bn), lambda j: (0, j)),
        out_shape=jax.ShapeDtypeStruct((rows, n), F32),
        name="adaln",
    )(cc, w, b.reshape(1, n))


def _dwconv(u, w, pos, row_w):
    taps = w.shape[0]
    p = taps // 2
    n = u.shape[0]
    out = u * w[p:p + 1, :]
    for k in range(taps):
        d = k - p
        if d == 0:
            continue
        shifted = pltpu.roll(u, (-d) % n, axis=0)
        valid = jnp.logical_and(pos + d >= 0, pos + d < row_w)
        out = out + jnp.where(valid, shifted, 0.0) * w[k:k + 1, :]
    return out


def _in_proj_kernel(x_ref, sh_ref, sc_ref, gpre_ref, wz_ref, wxbc_ref, wdt_ref, wdtT_ref, wgb_ref, wgc_ref,
                    wv_ref, cw_ref, cb_ref, dtb_ref, dtbT_ref, scw_ref, scg_ref,
                    z_ref, xbc_ref, dt_ref, dtT_ref, sco_ref, *, row_w):
    x = x_ref[0]
    tm = x.shape[0]
    h = _rms(x) * gpre_ref[...]
    h = h * (1.0 + sc_ref[0]) + sh_ref[0]
    hb = h.astype(MXU_DTYPE)
    pos = lax.broadcasted_iota(jnp.int32, (tm, 1), 0) % row_w

    z_ref[0] = _dot(hb, wz_ref[...])
    xbc = _dwconv(_dot(hb, wxbc_ref[...]), cw_ref[...], pos, row_w) + cb_ref[...]
    xbc_ref[0] = _silu(xbc)

    nh = dt_ref.shape[-1]
    dt = _softplus(_dot(hb, wdt_ref[...]) + dtb_ref[...])
    dt_ref[0, 0] = dt[:, :nh]
    dt_ref[1, 0] = dt[:, nh:]
    dtT = _softplus(_dot_nt(wdtT_ref[...], hb) + dtbT_ref[...])
    dtT_ref[0, 0] = dtT[:nh]
    dtT_ref[1, 0] = dtT[nh:]

    gcv = _dot(hb, wgc_ref[...]) * _dot(hb, wv_ref[...])
    sc = _dot(hb, wgb_ref[...]) * _dwconv(gcv, scw_ref[...], pos, row_w)
    sco_ref[0] = _rms(sc) * scg_ref[...]


def _in_proj(x, sh, sc, g_pre, w, conv_w, conv_b, dt_bias, sc_conv_w, sc_norm_g, row_w):
    B, L, D = x.shape
    tm = min(PROJ_TOKENS, L)
    assert L % tm == 0 and tm % row_w == 0
    wz, wxbc, wdt, wgb, wgc, wv = w
    d_ssd, xbc_dim, nh2, d_conv = wz.shape[1], wxbc.shape[1], wdt.shape[1], wgb.shape[1]
    nh = nh2 // 2
    tok = lambda cols: pl.BlockSpec((1, tm, cols), lambda b, i: (b, i, 0))
    per_b = pl.BlockSpec((1, 1, D), lambda b, i: (b, 0, 0))
    return pl.pallas_call(
        functools.partial(_in_proj_kernel, row_w=row_w),
        grid=(B, L // tm),
        in_specs=[tok(D), per_b, per_b, _const_spec((1, D)),
                  _const_spec(wz.shape), _const_spec(wxbc.shape), _const_spec(wdt.shape),
                  _const_spec((nh2, D)), _const_spec(wgb.shape), _const_spec(wgc.shape), _const_spec(wv.shape),
                  _const_spec(conv_w.shape), _const_spec((1, xbc_dim)), _const_spec((1, nh2)),
                  _const_spec((nh2, 1)), _const_spec(sc_conv_w.shape), _const_spec((1, d_conv))],
        out_specs=[tok(d_ssd), tok(xbc_dim),
                   pl.BlockSpec((2, 1, tm, nh), lambda b, i: (0, b, i, 0)),
                   pl.BlockSpec((2, 1, nh, tm), lambda b, i: (0, b, 0, i)),
                   tok(d_conv)],
        out_shape=[jax.ShapeDtypeStruct((B, L, d_ssd), F32),
                   jax.ShapeDtypeStruct((B, L, xbc_dim), F32),
                   jax.ShapeDtypeStruct((2, B, L, nh), F32),
                   jax.ShapeDtypeStruct((2, B, nh, L), F32),
                   jax.ShapeDtypeStruct((B, L, d_conv), F32)],
        compiler_params=pltpu.CompilerParams(dimension_semantics=("parallel", "parallel"),
                                             vmem_limit_bytes=VMEM_LIMIT_BYTES),
        name="in_proj",
    )(x, sh, sc, g_pre.reshape(1, D), wz, wxbc, wdt, wdt.T, wgb, wgc, wv,
      conv_w, conv_b.reshape(1, xbc_dim), dt_bias.reshape(1, nh2), dt_bias.reshape(nh2, 1),
      sc_conv_w, sc_norm_g.reshape(1, d_conv))


def _ssd_kernel(xs_ref, bm_ref, cm_ref, dt_ref, dtT_ref, alog_ref, alogT_ref, h0_ref, y_ref, hf_ref, st_ref):
    d = pl.program_id(1)
    c = pl.program_id(2)
    nc = pl.num_programs(2)
    Q = SSD_CHUNK
    N = SSD_STATE
    P = SSD_HEAD_DIM
    d_ssd = xs_ref.shape[-1]
    nh = dt_ref.shape[-1]
    hpg = nh // SSD_GROUPS
    gw = hpg * P

    @pl.when(c == 0)
    def _():
        st_ref[...] = h0_ref[0, 0]

    a = dt_ref[0, 0] * (-jnp.exp(alog_ref[0]))
    aT = dtT_ref[0, 0] * (-jnp.exp(alogT_ref[0]))
    row = lax.broadcasted_iota(jnp.int32, (Q, Q), 0)
    col = lax.broadcasted_iota(jnp.int32, (Q, Q), 1)
    fwd = d == 0
    mask = (row - col) * (1 - 2 * d) >= 0
    maskb = mask.astype(F32).astype(MXU_DTYPE)
    acs = sum(_dot(maskb, p) for p in _split_bf16(a, 3))
    acsT = sum(_dot_nt(p, maskb) for p in _split_bf16(aT, 3))

    erow = lax.broadcasted_iota(jnp.int32, (nh, d_ssd), 0)
    ecol = lax.broadcasted_iota(jnp.int32, (nh, d_ssd), 1)
    expand = (ecol // P == erow).astype(F32).astype(MXU_DTYPE)
    acs_e = sum(_dot(p, expand) for p in _split_bf16(acs, 2))
    dt_e = sum(_dot(p, expand) for p in _split_bf16(dt_ref[0, 0], 2))
    tot_e = jnp.where(fwd, acs_e[Q - 1:Q, :], acs_e[0:1, :])

    xdt = xs_ref[0] * dt_e
    xw = (xdt * jnp.exp(tot_e - acs_e)).astype(MXU_DTYPE)
    xdtb = xdt.astype(MXU_DTYPE)
    eacs = jnp.exp(acs_e)
    cdec = jnp.exp(tot_e)
    lane = lax.broadcasted_iota(jnp.int32, (Q, 2 * P), 1)

    for g in range(SSD_GROUPS):
        bm = bm_ref[0, :, g * N:(g + 1) * N].astype(MXU_DTYPE)
        cm = cm_ref[0, :, g * N:(g + 1) * N].astype(MXU_DTYPE)
        cb = _dot_nt(cm, bm)
        hT = st_ref[g * N:(g + 1) * N, :]
        gcols = slice(g * gw, (g + 1) * gw)
        y_off = _dot(cm, hT.astype(MXU_DTYPE)) * eacs[:, gcols]
        st_ref[g * N:(g + 1) * N, :] = hT * cdec[:, gcols] + _dot_tn(bm, xw[:, gcols])
        for pr in range(hpg // 2):
            pcols = slice(g * gw + pr * 2 * P, g * gw + (pr + 1) * 2 * P)
            xpair = xdtb[:, pcols]
            halves = []
            for r in range(2):
                hd = g * hpg + pr * 2 + r
                seg = acs[:, hd:hd + 1] - acsT[hd:hd + 1, :]
                decay = jnp.exp(jnp.where(mask, seg, -jnp.inf))
                halves.append(_dot((cb * decay).astype(MXU_DTYPE), xpair))
            y_diag = jnp.where(lane < P, halves[0], halves[1])
            y_ref[0, 0, :, pcols] = y_diag + y_off[:, pr * 2 * P:(pr + 1) * 2 * P]

    @pl.when(c == nc - 1)
    def _():
        hf_ref[0, 0] = st_ref[...]


def _ssd_scan(xbc, dt, dtT, a_log, h0):
    B, L, _ = xbc.shape
    nh = dt.shape[-1]
    d_ssd = nh * SSD_HEAD_DIM
    gn = SSD_GROUPS * SSD_STATE
    Q = SSD_CHUNK
    nc = L // Q
    gw = d_ssd // SSD_GROUPS
    chunk = lambda d, c: c + d * (nc - 1 - 2 * c)
    return pl.pallas_call(
        _ssd_kernel,
        grid=(B, 2, nc),
        in_specs=[pl.BlockSpec((1, Q, d_ssd), lambda b, d, c: (b, chunk(d, c), 0)),
                  pl.BlockSpec((1, Q, gn), lambda b, d, c: (b, chunk(d, c), d_ssd // gn)),
                  pl.BlockSpec((1, Q, gn), lambda b, d, c: (b, chunk(d, c), d_ssd // gn + 1)),
                  pl.BlockSpec((1, 1, Q, nh), lambda b, d, c: (d, b, chunk(d, c), 0)),
                  pl.BlockSpec((1, 1, nh, Q), lambda b, d, c: (d, b, 0, chunk(d, c))),
                  pl.BlockSpec((1, 1, nh), lambda b, d, c: (d, 0, 0)),
                  pl.BlockSpec((1, nh, 1), lambda b, d, c: (d, 0, 0)),
                  pl.BlockSpec((1, 1, gn, gw), lambda b, d, c: (b, d, 0, 0))],
        out_specs=[pl.BlockSpec((1, 1, Q, d_ssd), lambda b, d, c: (d, b, chunk(d, c), 0)),
                   pl.BlockSpec((1, 1, gn, gw), lambda b, d, c: (b, d, 0, 0))],
        out_shape=[jax.ShapeDtypeStruct((2, B, L, d_ssd), F32),
                   jax.ShapeDtypeStruct((B, 2, gn, gw), F32)],
        scratch_shapes=[pltpu.VMEM((gn, gw), F32)],
        compiler_params=pltpu.CompilerParams(dimension_semantics=("parallel", "arbitrary", "arbitrary"),
                                             vmem_limit_bytes=VMEM_LIMIT_BYTES),
        name="ssd_scan",
    )(xbc, xbc, xbc, dt, dtT, a_log.reshape(2, 1, nh), a_log.reshape(2, nh, 1), h0)


def _mix_out_kernel(x_ref, yf_ref, yb_ref, xs_ref, z_ref, sco_ref, dsk_ref, ng_ref, wo1_ref, wo2_ref,
                    gpost_ref, gt_ref, gpre_ref, sh_ref, sc_ref, x1_ref, h2_ref):
    d_ssd = xs_ref.shape[-1]
    gw = d_ssd // SSD_GROUPS
    y = yf_ref[0, 0] + yb_ref[0, 0] + dsk_ref[...] * xs_ref[0]
    y = y * _silu(z_ref[0])
    parts = [_rms(y[:, g * gw:(g + 1) * gw]) for g in range(SSD_GROUPS)]
    yn = jnp.concatenate(parts, axis=-1) * ng_ref[...]
    mix = _dot(yn.astype(MXU_DTYPE), wo1_ref[...]) + _dot(sco_ref[0].astype(MXU_DTYPE), wo2_ref[...])
    x1 = x_ref[0] + gt_ref[0] * (_rms(mix) * gpost_ref[...])
    x1_ref[0] = x1
    h2 = _rms(x1) * gpre_ref[...]
    h2_ref[0] = h2 * (1.0 + sc_ref[0]) + sh_ref[0]


def _mix_out(x, y2, xbc, z, sco, d_skip_e, norm_g, wo1, wo2, g_post, gt1, g_pre_ffn, sh2, sc2):
    B, L, D = x.shape
    d_ssd = z.shape[-1]
    d_conv = sco.shape[-1]
    tm = min(PROJ_TOKENS, L)
    tok = lambda cols: pl.BlockSpec((1, tm, cols), lambda b, i: (b, i, 0))
    per_b = pl.BlockSpec((1, 1, D), lambda b, i: (b, 0, 0))
    ydir = lambda d: pl.BlockSpec((1, 1, tm, d_ssd), lambda b, i: (d, b, i, 0))
    return pl.pallas_call(
        _mix_out_kernel,
        grid=(B, L // tm),
        in_specs=[tok(D), ydir(0), ydir(1), tok(d_ssd), tok(d_ssd), tok(d_conv),
                  _const_spec((1, d_ssd)), _const_spec((1, d_ssd)), _const_spec(wo1.shape), _const_spec(wo2.shape),
                  _const_spec((1, D)), per_b, _const_spec((1, D)), per_b, per_b],
        out_specs=[tok(D), tok(D)],
        out_shape=[jax.ShapeDtypeStruct((B, L, D), F32), jax.ShapeDtypeStruct((B, L, D), F32)],
        compiler_params=pltpu.CompilerParams(dimension_semantics=("parallel", "parallel"),
                                             vmem_limit_bytes=VMEM_LIMIT_BYTES),
        name="mix_out",
    )(x, y2, y2, xbc, z, sco, d_skip_e.reshape(1, d_ssd), norm_g.reshape(1, d_ssd), wo1, wo2,
      g_post.reshape(1, D), gt1, g_pre_ffn.reshape(1, D), sh2, sc2)


def _topk_rows(s, k, fill):
    n = s.shape[0]
    idx_iota = lax.broadcasted_iota(jnp.int32, s.shape, 0)
    vals, idxs = [], []
    for _ in range(k):
        m = jnp.max(s, axis=0, keepdims=True)
        i = jnp.min(jnp.where(s == m, idx_iota, n), axis=0, keepdims=True)
        vals.append(m)
        idxs.append(i)
        s = jnp.where(idx_iota == i, fill, s)
    return vals, idxs


def _route_kernel(h2_ref, wq_ref, keys_ref, eid_ref, g_ref):
    K = PEER_TOPK
    NK = PEER_N_KEYS
    hb = h2_ref[...].astype(MXU_DTYPE)
    T = hb.shape[0]
    half = keys_ref.shape[-1]

    def head(h, carry):
        q = _dot(hb, wq_ref[h]).astype(MXU_DTYPE)
        sv, si = [], []
        for p in range(2):
            s = _dot_nt(keys_ref[h, p], q[:, p * half:(p + 1) * half])
            v, i = _topk_rows(s, K, -jnp.inf)
            sv.append(v)
            si.append(i)
        sv1 = jnp.concatenate(sv[1], axis=0)
        si1 = jnp.concatenate(si[1], axis=0)
        cand = jnp.concatenate([sv[0][i] + sv1 for i in range(K)], axis=0)
        ecand = jnp.concatenate([si[0][i] * NK + si1 for i in range(K)], axis=0)
        c_iota = lax.broadcasted_iota(jnp.int32, cand.shape, 0)
        cv, eid = [], []
        for _ in range(K):
            m = jnp.max(cand, axis=0, keepdims=True)
            i = jnp.min(jnp.where(cand == m, c_iota, K * K), axis=0, keepdims=True)
            hit = c_iota == i
            cv.append(m)
            eid.append(jnp.max(jnp.where(hit, ecand, -1), axis=0, keepdims=True))
            cand = jnp.where(hit, -jnp.inf, cand)
        cvs = jnp.concatenate(cv, axis=0)
        ex = jnp.exp(cvs - cv[0])
        gate = ex / jnp.sum(ex, axis=0, keepdims=True)
        rows = pl.ds(pl.multiple_of(h * K, K), K)
        eid_ref[rows, :] = jnp.concatenate(eid, axis=0)
        g_ref[rows, :] = gate
        return carry

    lax.fori_loop(0, PEER_HEADS, head, 0)


def _peer_route(h2, wq, keys):
    ntok, D = h2.shape
    T = ROUTE_TOKENS
    slots = PEER_HEADS * PEER_TOPK
    return pl.pallas_call(
        _route_kernel,
        grid=(ntok // T,),
        in_specs=[pl.BlockSpec((T, D), lambda i: (i, 0)), _const_spec(wq.shape), _const_spec(keys.shape)],
        out_specs=[pl.BlockSpec((slots, T), lambda i: (0, i)), pl.BlockSpec((slots, T), lambda i: (0, i))],
        out_shape=[jax.ShapeDtypeStruct((slots, ntok), jnp.int32), jax.ShapeDtypeStruct((slots, ntok), F32)],
        compiler_params=pltpu.CompilerParams(dimension_semantics=("parallel",),
                                             vmem_limit_bytes=VMEM_LIMIT_BYTES),
        name="peer_route",
    )(h2, wq, keys)


def _peer_kernel(eid_hbm, tab_hbm, x_ref, g_ref, o_ref, eid_s, buf, rsem, esem):
    i = pl.program_id(0)
    n = pl.num_programs(0)
    T, D = x_ref.shape
    S = g_ref.shape[-1]
    blk = T * S

    def eid_copy(step, slot):
        return pltpu.make_async_copy(eid_hbm.at[pl.ds(step * blk, blk)],
                                     eid_s.at[pl.ds(slot * blk, blk)], esem.at[slot])

    es = i % 2

    @pl.when(i == 0)
    def _():
        eid_copy(0, 0).start()

    eid_copy(i, es).wait()

    @pl.when(i + 1 < n)
    def _():
        eid_copy(i + 1, 1 - es).start()

    ebase = es * blk

    def issue(t):
        slot = t % PEER_RING
        for k in range(S):
            e = eid_s[ebase + t * S + k]
            pltpu.make_async_copy(tab_hbm.at[pl.ds(e, 1), :], buf.at[slot, pl.ds(k, 1), :], rsem.at[slot]).start()

    def wait(t):
        slot = t % PEER_RING
        pltpu.make_async_copy(tab_hbm.at[pl.ds(0, S), :], buf.at[slot], rsem.at[slot]).wait()

    for t in range(PEER_RING - 1):
        issue(t)

    eye = lax.broadcasted_iota(jnp.int32, (S, S), 0) == lax.broadcasted_iota(jnp.int32, (S, S), 1)

    def body(t, carry):
        @pl.when(t + PEER_RING - 1 < T)
        def _():
            issue(t + PEER_RING - 1)

        wait(t)
        slot = t % PEER_RING
        x_row = x_ref[pl.ds(t, 1), :]
        g_col = jnp.sum(jnp.where(eye, g_ref[pl.ds(t, 1), :], 0.0), axis=-1, keepdims=True)
        act = jnp.sum(buf[slot, :, 0:D] * x_row, axis=-1, keepdims=True)
        w = g_col * _gelu_tanh(act)
        o_ref[pl.ds(t, 1), :] = jnp.sum(w * buf[slot, :, D:2 * D], axis=0, keepdims=True)
        return carry

    lax.fori_loop(0, T, body, 0)


def _peer_experts(eid_flat, g, h2, tab):
    ntok, D = h2.shape
    S = g.shape[-1]
    T = PEER_TOK_BLOCK
    return pl.pallas_call(
        _peer_kernel,
        grid=(ntok // T,),
        in_specs=[pl.BlockSpec(memory_space=pl.ANY),
                  pl.BlockSpec(memory_space=pl.ANY),
                  pl.BlockSpec((T, D), lambda i: (i, 0)),
                  pl.BlockSpec((T, S), lambda i: (i, 0))],
        out_specs=pl.BlockSpec((T, D), lambda i: (i, 0)),
        out_shape=jax.ShapeDtypeStruct((ntok, D), F32),
        scratch_shapes=[pltpu.SMEM((2 * T * S,), jnp.int32),
                        pltpu.VMEM((PEER_RING, S, 2 * D), F32),
                        pltpu.SemaphoreType.DMA((PEER_RING,)),
                        pltpu.SemaphoreType.DMA((2,))],
        compiler_params=pltpu.CompilerParams(dimension_semantics=("arbitrary",)),
        name="peer_experts",
    )(eid_flat, tab, h2, g)


def _final_kernel(x1_ref, p_ref, gt_ref, gpost_ref, o_ref):
    o_ref[0] = x1_ref[0] + gt_ref[0] * (_rms(p_ref[0]) * gpost_ref[...])


def _final(x1, peer_out, gt2, g_post):
    B, L, D = x1.shape
    tm = min(2 * PROJ_TOKENS, L)
    tok = pl.BlockSpec((1, tm, D), lambda b, i: (b, i, 0))
    return pl.pallas_call(
        _final_kernel,
        grid=(B, L // tm),
        in_specs=[tok, tok, pl.BlockSpec((1, 1, D), lambda b, i: (b, 0, 0)), _const_spec((1, D))],
        out_specs=tok,
        out_shape=jax.ShapeDtypeStruct((B, L, D), F32),
        compiler_params=pltpu.CompilerParams(dimension_semantics=("parallel", "parallel")),
        name="final",
    )(x1, peer_out, gt2, g_post.reshape(1, D))


def _split_w_in(w_in, d_ssd, xbc_dim, nh2, d_conv):
    bounds = [0, d_ssd, d_ssd + xbc_dim, d_ssd + xbc_dim + nh2]
    bounds += [bounds[-1] + d_conv, bounds[-1] + 2 * d_conv, bounds[-1] + 3 * d_conv]
    wz, wxbc, wdt, wgb, wgc, wv = (w_in[:, bounds[i]:bounds[i + 1]].astype(MXU_DTYPE) for i in range(6))
    return wz, wxbc, wdt, wgb, wgc, wv


def kernel(x, c, ctx, c_ctx, w_ada, b_ada, g_pre_mix, g_post_mix, g_pre_ffn, g_post_ffn, w_in, ssd_conv_w, ssd_conv_b, ssd_dt_bias, ssd_a_log, ssd_d, ssd_norm_g, sc_conv_w, sc_norm_g, w_out, peer_w_q, peer_keys, peer_u, peer_v):
    B, L, D = x.shape
    ctx_len = ctx.shape[1]
    assert w_ada.shape[0] == 1, "single-layer block"
    nh2 = ssd_dt_bias.shape[1] * ssd_dt_bias.shape[2]
    nh = nh2 // 2
    d_ssd = nh * SSD_HEAD_DIM
    xbc_dim = ssd_conv_w.shape[-1]
    d_conv = sc_conv_w.shape[-1]
    n_mod = w_ada.shape[-1] // D

    pad_rows = (-(B + 1)) % 8
    cc = jnp.concatenate([c, c_ctx[None, :], jnp.zeros((pad_rows, D), F32)], axis=0)
    mod = _adaln(cc, w_ada[0], b_ada[0])
    mod_l = mod[:B].reshape(B, n_mod, 1, D)
    mod_c = jnp.broadcast_to(mod[B].reshape(1, n_mod, 1, D), (B, n_mod, 1, D))
    sh1_l, sc1_l, gt1_l, sh2_l, sc2_l, gt2_l = (mod_l[:, j] for j in range(n_mod))
    sh1_c, sc1_c = mod_c[:, 0], mod_c[:, 1]

    w = _split_w_in(w_in[0], d_ssd, xbc_dim, nh2, d_conv)
    proj_args = (w, ssd_conv_w[0], ssd_conv_b[0], ssd_dt_bias[0].reshape(nh2), sc_conv_w[0], sc_norm_g[0])
    a_log = ssd_a_log[0]

    _, xbc_c, dt_c, dtT_c, _ = _in_proj(ctx, sh1_c, sc1_c, g_pre_mix[0], *proj_args, row_w=ctx_len)
    gn = SSD_GROUPS * SSD_STATE
    h0 = jnp.zeros((B, 2, gn, d_ssd // SSD_GROUPS), F32)
    _, h_ctx = _ssd_scan(xbc_c, dt_c, dtT_c, a_log, h0)

    z_l, xbc_l, dt_l, dtT_l, sco_l = _in_proj(x, sh1_l, sc1_l, g_pre_mix[0], *proj_args, row_w=GRID_W)
    y2, _ = _ssd_scan(xbc_l, dt_l, dtT_l, a_log, h_ctx)

    wo = w_out[0].astype(MXU_DTYPE)
    d_skip_e = jnp.repeat(ssd_d[0], SSD_HEAD_DIM)
    x1, h2 = _mix_out(x, y2, xbc_l, z_l, sco_l, d_skip_e, ssd_norm_g[0], wo[:d_ssd], wo[d_ssd:],
                      g_post_mix[0], gt1_l, g_pre_ffn[0], sh2_l, sc2_l)

    ntok = B * L
    qd = peer_w_q.shape[-1] // PEER_HEADS
    wq = peer_w_q[0].reshape(D, PEER_HEADS, qd).transpose(1, 0, 2).astype(MXU_DTYPE)
    eidT, gT = _peer_route(h2.reshape(ntok, D), wq, peer_keys[0].astype(MXU_DTYPE))
    tab = jnp.concatenate([peer_u[0], peer_v[0]], axis=1)
    peer_out = _peer_experts(eidT.T.reshape(-1), gT.T, h2.reshape(ntok, D), tab)

    return _final(x1, peer_out.reshape(B, L, D), gt2_l, g_post_ffn[0])
```

```python
import functools

import jax
import jax.numpy as jnp
from jax import lax
from jax.experimental import pallas as pl
from jax.experimental.pallas import tpu as pltpu

GRID_W = 64
SSD_HEAD_DIM = 64
SSD_GROUPS = 4
SSD_STATE = 128
SSD_CHUNK = 128
PEER_HEADS = 8
PEER_N_KEYS = 128
PEER_TOPK = 16
EPS = 1e-6
LANES = 128

MXU_DTYPE = jnp.bfloat16
VMEM_LIMIT_BYTES = 56 * 1024 * 1024

PROJ_TOKENS = 256
ROUTE_TOKENS = 256
PEER_TOK_BLOCK = 128
PEER_RING = 8

F32 = jnp.float32


def _dot(a, b):
    return jnp.dot(a, b, preferred_element_type=F32)


def _dot_nt(a, b):
    return lax.dot_general(a, b, (((1,), (1,)), ((), ())), preferred_element_type=F32)


def _dot_tn(a, b):
    return lax.dot_general(a, b, (((0,), (0,)), ((), ())), preferred_element_type=F32)


def _split_bf16(a, terms):
    parts = []
    r = a
    for i in range(terms):
        p = r.astype(MXU_DTYPE)
        parts.append(p)
        if i + 1 < terms:
            r = r - p.astype(F32)
    return parts


def _silu(x):
    return x * jax.nn.sigmoid(x)


def _softplus(x):
    return jnp.maximum(x, 0.0) + jnp.log1p(jnp.exp(-jnp.abs(x)))


def _gelu_tanh(x):
    return 0.5 * x * (1.0 + jnp.tanh(0.7978845608028654 * (x + 0.044715 * (x * x * x))))


def _rms(x):
    return x * lax.rsqrt(jnp.mean(x * x, axis=-1, keepdims=True) + EPS)


def _const_spec(shape):
    zeros = (0,) * len(shape)
    return pl.BlockSpec(shape, lambda *_: zeros, pipeline_mode=pl.Buffered(1))


def _adaln_kernel(c_ref, w_ref, b_ref, o_ref):
    s = _silu(c_ref[...]).astype(MXU_DTYPE)
    o_ref[...] = _dot(s, w_ref[...].astype(MXU_DTYPE)) + b_ref[...]


def _adaln(cc, w, b):
    rows, d = cc.shape
    n = w.shape[1]
    bn = 1024
    return pl.pallas_call(
        _adaln_kernel,
        grid=(n // bn,),
        in_specs=[pl.BlockSpec((rows, d), lambda j: (0, 0)),
                  pl.BlockSpec((d, bn), lambda j: (0, j)),
                  pl.BlockSpec((1, bn), lambda j: (0, j))],
        out_specs=pl.BlockSpec((rows, bn), lambda j: (0, j)),
        out_shape=jax.ShapeDtypeStruct((rows, n), F32),
        name="adaln",
    )(cc, w, b.reshape(1, n))


def _dwconv(u, w, pos, row_w):
    taps = w.shape[0]
    p = taps // 2
    n = u.shape[0]
    out = u * w[p:p + 1, :]
    for k in range(taps):
        d = k - p
        if d == 0:
            continue
        shifted = pltpu.roll(u, (-d) % n, axis=0)
        valid = jnp.logical_and(pos + d >= 0, pos + d < row_w)
        out = out + jnp.where(valid, shifted, 0.0) * w[k:k + 1, :]
    return out


def _in_proj_kernel(x_ref, sh_ref, sc_ref, gpre_ref, wz_ref, wxbc_ref, wdt_ref, wdtT_ref, wgb_ref, wgc_ref,
                    wv_ref, cw_ref, cb_ref, dtb_ref, dtbT_ref, scw_ref, scg_ref,
                    z_ref, xbc_ref, dt_ref, dtT_ref, sco_ref, *, row_w):
    x = x_ref[0]
    tm = x.shape[0]
    h = _rms(x) * gpre_ref[...]
    h = h * (1.0 + sc_ref[0]) + sh_ref[0]
    hb = h.astype(MXU_DTYPE)
    pos = lax.broadcasted_iota(jnp.int32, (tm, 1), 0) % row_w

    z_ref[0] = _dot(hb, wz_ref[...])
    xbc = _dwconv(_dot(hb, wxbc_ref[...]), cw_ref[...], pos, row_w) + cb_ref[...]
    xbc_ref[0] = _silu(xbc)

    nh = dt_ref.shape[-1]
    dt = _softplus(_dot(hb, wdt_ref[...]) + dtb_ref[...])
    dt_ref[0, 0] = dt[:, :nh]
    dt_ref[1, 0] = dt[:, nh:]
    dtT = _softplus(_dot_nt(wdtT_ref[...], hb) + dtbT_ref[...])
    dtT_ref[0, 0] = dtT[:nh]
    dtT_ref[1, 0] = dtT[nh:]

    gcv = _dot(hb, wgc_ref[...]) * _dot(hb, wv_ref[...])
    sc = _dot(hb, wgb_ref[...]) * _dwconv(gcv, scw_ref[...], pos, row_w)
    sco_ref[0] = _rms(sc) * scg_ref[...]


def _in_proj(x, sh, sc, g_pre, w, conv_w, conv_b, dt_bias, sc_conv_w, sc_norm_g, row_w):
    B, L, D = x.shape
    tm = min(PROJ_TOKENS, L)
    assert L % tm == 0 and tm % row_w == 0
    wz, wxbc, wdt, wgb, wgc, wv = w
    d_ssd, xbc_dim, nh2, d_conv = wz.shape[1], wxbc.shape[1], wdt.shape[1], wgb.shape[1]
    nh = nh2 // 2
    tok = lambda cols: pl.BlockSpec((1, tm, cols), lambda b, i: (b, i, 0))
    per_b = pl.BlockSpec((1, 1, D), lambda b, i: (b, 0, 0))
    return pl.pallas_call(
        functools.partial(_in_proj_kernel, row_w=row_w),
        grid=(B, L // tm),
        in_specs=[tok(D), per_b, per_b, _const_spec((1, D)),
                  _const_spec(wz.shape), _const_spec(wxbc.shape), _const_spec(wdt.shape),
                  _const_spec((nh2, D)), _const_spec(wgb.shape), _const_spec(wgc.shape), _const_spec(wv.shape),
                  _const_spec(conv_w.shape), _const_spec((1, xbc_dim)), _const_spec((1, nh2)),
                  _const_spec((nh2, 1)), _const_spec(sc_conv_w.shape), _const_spec((1, d_conv))],
        out_specs=[tok(d_ssd), tok(xbc_dim),
                   pl.BlockSpec((2, 1, tm, nh), lambda b, i: (0, b, i, 0)),
                   pl.BlockSpec((2, 1, nh, tm), lambda b, i: (0, b, 0, i)),
                   tok(d_conv)],
        out_shape=[jax.ShapeDtypeStruct((B, L, d_ssd), F32),
                   jax.ShapeDtypeStruct((B, L, xbc_dim), F32),
                   jax.ShapeDtypeStruct((2, B, L, nh), F32),
                   jax.ShapeDtypeStruct((2, B, nh, L), F32),
                   jax.ShapeDtypeStruct((B, L, d_conv), F32)],
        compiler_params=pltpu.CompilerParams(dimension_semantics=("parallel", "parallel"),
                                             vmem_limit_bytes=VMEM_LIMIT_BYTES),
        name="in_proj",
    )(x, sh, sc, g_pre.reshape(1, D), wz, wxbc, wdt, wdt.T, wgb, wgc, wv,
      conv_w, conv_b.reshape(1, xbc_dim), dt_bias.reshape(1, nh2), dt_bias.reshape(nh2, 1),
      sc_conv_w, sc_norm_g.reshape(1, d_conv))


def _ssd_kernel(xs_ref, bm_ref, cm_ref, dt_ref, dtT_ref, alog_ref, alogT_ref, h0_ref, y_ref, hf_ref, st_ref):
    d = pl.program_id(1)
    c = pl.program_id(2)
    nc = pl.num_programs(2)
    Q = SSD_CHUNK
    N = SSD_STATE
    P = SSD_HEAD_DIM
    d_ssd = xs_ref.shape[-1]
    nh = dt_ref.shape[-1]
    hpg = nh // SSD_GROUPS
    gw = hpg * P

    @pl.when(c == 0)
    def _():
        st_ref[...] = h0_ref[0, 0]

    a = dt_ref[0, 0] * (-jnp.exp(alog_ref[0]))
    aT = dtT_ref[0, 0] * (-jnp.exp(alogT_ref[0]))
    row = lax.broadcasted_iota(jnp.int32, (Q, Q), 0)
    col = lax.broadcasted_iota(jnp.int32, (Q, Q), 1)
    fwd = d == 0
    mask = (row - col) * (1 - 2 * d) >= 0
    maskb = mask.astype(F32).astype(MXU_DTYPE)
    acs = sum(_dot(maskb, p) for p in _split_bf16(a, 3))
    acsT = sum(_dot_nt(p, maskb) for p in _split_bf16(aT, 3))

    erow = lax.broadcasted_iota(jnp.int32, (nh, d_ssd), 0)
    ecol = lax.broadcasted_iota(jnp.int32, (nh, d_ssd), 1)
    expand = (ecol // P == erow).astype(F32).astype(MXU_DTYPE)
    acs_e = sum(_dot(p, expand) for p in _split_bf16(acs, 2))
    dt_e = sum(_dot(p, expand) for p in _split_bf16(dt_ref[0, 0], 2))
    tot_e = jnp.where(fwd, acs_e[Q - 1:Q, :], acs_e[0:1, :])

    xdt = xs_ref[0] * dt_e
    xw = (xdt * jnp.exp(tot_e - acs_e)).astype(MXU_DTYPE)
    xdtb = xdt.astype(MXU_DTYPE)
    eacs = jnp.exp(acs_e)
    cdec = jnp.exp(tot_e)
    lane = lax.broadcasted_iota(jnp.int32, (Q, 2 * P), 1)

    for g in range(SSD_GROUPS):
        bm = bm_ref[0, :, g * N:(g + 1) * N].astype(MXU_DTYPE)
        cm = cm_ref[0, :, g * N:(g + 1) * N].astype(MXU_DTYPE)
        cb = _dot_nt(cm, bm)
        hT = st_ref[g * N:(g + 1) * N, :]
        gcols = slice(g * gw, (g + 1) * gw)
        y_off = _dot(cm, hT.astype(MXU_DTYPE)) * eacs[:, gcols]
        st_ref[g * N:(g + 1) * N, :] = hT * cdec[:, gcols] + _dot_tn(bm, xw[:, gcols])
        for pr in range(hpg // 2):
            pcols = slice(g * gw + pr * 2 * P, g * gw + (pr + 1) * 2 * P)
            xpair = xdtb[:, pcols]
            halves = []
            for r in range(2):
                hd = g * hpg + pr * 2 + r
                seg = acs[:, hd:hd + 1] - acsT[hd:hd + 1, :]
                decay = jnp.exp(jnp.where(mask, seg, -jnp.inf))
                halves.append(_dot((cb * decay).astype(MXU_DTYPE), xpair))
            y_diag = jnp.where(lane < P, halves[0], halves[1])
            y_ref[0, 0, :, pcols] = y_diag + y_off[:, pr * 2 * P:(pr + 1) * 2 * P]

    @pl.when(c == nc - 1)
    def _():
        hf_ref[0, 0] = st_ref[...]


def _ssd_scan(xbc, dt, dtT, a_log, h0):
    B, L, _ = xbc.shape
    nh = dt.shape[-1]
    d_ssd = nh * SSD_HEAD_DIM
    gn = SSD_GROUPS * SSD_STATE
    Q = SSD_CHUNK
    nc = L // Q
    gw = d_ssd // SSD_GROUPS
    chunk = lambda d, c: c + d * (nc - 1 - 2 * c)
    return pl.pallas_call(
        _ssd_kernel,
        grid=(B, 2, nc),
        in_specs=[pl.BlockSpec((1, Q, d_ssd), lambda b, d, c: (b, chunk(d, c), 0)),
                  pl.BlockSpec((1, Q, gn), lambda b, d, c: (b, chunk(d, c), d_ssd // gn)),
                  pl.BlockSpec((1, Q, gn), lambda b, d, c: (b, chunk(d, c), d_ssd // gn + 1)),
                  pl.BlockSpec((1, 1, Q, nh), lambda b, d, c: (d, b, chunk(d, c), 0)),
                  pl.BlockSpec((1, 1, nh, Q), lambda b, d, c: (d, b, 0, chunk(d, c))),
                  pl.BlockSpec((1, 1, nh), lambda b, d, c: (d, 0, 0)),
                  pl.BlockSpec((1, nh, 1), lambda b, d, c: (d, 0, 0)),
                  pl.BlockSpec((1, 1, gn, gw), lambda b, d, c: (b, d, 0, 0))],
        out_specs=[pl.BlockSpec((1, 1, Q, d_ssd), lambda b, d, c: (d, b, chunk(d, c), 0)),
                   pl.BlockSpec((1, 1, gn, gw), lambda b, d, c: (b, d, 0, 0))],
        out_shape=[jax.ShapeDtypeStruct((2, B, L, d_ssd), F32),
                   jax.ShapeDtypeStruct((B, 2, gn, gw), F32)],
        scratch_shapes=[pltpu.VMEM((gn, gw), F32)],
        compiler_params=pltpu.CompilerParams(dimension_semantics=("parallel", "arbitrary", "arbitrary"),
                                             vmem_limit_bytes=VMEM_LIMIT_BYTES),
        name="ssd_scan",
    )(xbc, xbc, xbc, dt, dtT, a_log.reshape(2, 1, nh), a_log.reshape(2, nh, 1), h0)


def _mix_out_kernel(x_ref, yf_ref, yb_ref, xs_ref, z_ref, sco_ref, dsk_ref, ng_ref, wo1_ref, wo2_ref,
                    gpost_ref, gt_ref, gpre_ref, sh_ref, sc_ref, x1_ref, h2_ref):
    d_ssd = xs_ref.shape[-1]
    gw = d_ssd // SSD_GROUPS
    y = yf_ref[0, 0] + yb_ref[0, 0] + dsk_ref[...] * xs_ref[0]
    y = y * _silu(z_ref[0])
    parts = [_rms(y[:, g * gw:(g + 1) * gw]) for g in range(SSD_GROUPS)]
    yn = jnp.concatenate(parts, axis=-1) * ng_ref[...]
    mix = _dot(yn.astype(MXU_DTYPE), wo1_ref[...]) + _dot(sco_ref[0].astype(MXU_DTYPE), wo2_ref[...])
    x1 = x_ref[0] + gt_ref[0] * (_rms(mix) * gpost_ref[...])
    x1_ref[0] = x1
    h2 = _rms(x1) * gpre_ref[...]
    h2_ref[0] = h2 * (1.0 + sc_ref[0]) + sh_ref[0]


def _mix_out(x, y2, xbc, z, sco, d_skip_e, norm_g, wo1, wo2, g_post, gt1, g_pre_ffn, sh2, sc2):
    B, L, D = x.shape
    d_ssd = z.shape[-1]
    d_conv = sco.shape[-1]
    tm = min(PROJ_TOKENS, L)
    tok = lambda cols: pl.BlockSpec((1, tm, cols), lambda b, i: (b, i, 0))
    per_b = pl.BlockSpec((1, 1, D), lambda b, i: (b, 0, 0))
    ydir = lambda d: pl.BlockSpec((1, 1, tm, d_ssd), lambda b, i: (d, b, i, 0))
    return pl.pallas_call(
        _mix_out_kernel,
        grid=(B, L // tm),
        in_specs=[tok(D), ydir(0), ydir(1), tok(d_ssd), tok(d_ssd), tok(d_conv),
                  _const_spec((1, d_ssd)), _const_spec((1, d_ssd)), _const_spec(wo1.shape), _const_spec(wo2.shape),
                  _const_spec((1, D)), per_b, _const_spec((1, D)), per_b, per_b],
        out_specs=[tok(D), tok(D)],
        out_shape=[jax.ShapeDtypeStruct((B, L, D), F32), jax.ShapeDtypeStruct((B, L, D), F32)],
        compiler_params=pltpu.CompilerParams(dimension_semantics=("parallel", "parallel"),
                                             vmem_limit_bytes=VMEM_LIMIT_BYTES),
        name="mix_out",
    )(x, y2, y2, xbc, z, sco, d_skip_e.reshape(1, d_ssd), norm_g.reshape(1, d_ssd), wo1, wo2,
      g_post.reshape(1, D), gt1, g_pre_ffn.reshape(1, D), sh2, sc2)


def _topk_rows(s, k, fill):
    n = s.shape[0]
    idx_iota = lax.broadcasted_iota(jnp.int32, s.shape, 0)
    vals, idxs = [], []
    for _ in range(k):
        m = jnp.max(s, axis=0, keepdims=True)
        i = jnp.min(jnp.where(s == m, idx_iota, n), axis=0, keepdims=True)
        vals.append(m)
        idxs.append(i)
        s = jnp.where(idx_iota == i, fill, s)
    return vals, idxs


def _route_kernel(h2_ref, wq_ref, keys_ref, eid_ref, g_ref):
    K = PEER_TOPK
    NK = PEER_N_KEYS
    hb = h2_ref[...].astype(MXU_DTYPE)
    T = hb.shape[0]
    half = keys_ref.shape[-1]

    def head(h, carry):
        q = _dot(hb, wq_ref[h]).astype(MXU_DTYPE)
        sv, si = [], []
        for p in range(2):
            s = _dot_nt(keys_ref[h, p], q[:, p * half:(p + 1) * half])
            v, i = _topk_rows(s, K, -jnp.inf)
            sv.append(v)
            si.append(i)
        sv1 = jnp.concatenate(sv[1], axis=0)
        si1 = jnp.concatenate(si[1], axis=0)
        cand = jnp.concatenate([sv[0][i] + sv1 for i in range(K)], axis=0)
        ecand = jnp.concatenate([si[0][i] * NK + si1 for i in range(K)], axis=0)
        c_iota = lax.broadcasted_iota(jnp.int32, cand.shape, 0)
        cv, eid = [], []
        for _ in range(K):
            m = jnp.max(cand, axis=0, keepdims=True)
            i = jnp.min(jnp.where(cand == m, c_iota, K * K), axis=0, keepdims=True)
            hit = c_iota == i
            cv.append(m)
            eid.append(jnp.max(jnp.where(hit, ecand, -1), axis=0, keepdims=True))
            cand = jnp.where(hit, -jnp.inf, cand)
        cvs = jnp.concatenate(cv, axis=0)
        ex = jnp.exp(cvs - cv[0])
        gate = ex / jnp.sum(ex, axis=0, keepdims=True)
        rows = pl.ds(pl.multiple_of(h * K, K), K)
        eid_ref[rows, :] = jnp.concatenate(eid, axis=0)
        g_ref[rows, :] = gate
        return carry

    lax.fori_loop(0, PEER_HEADS, head, 0)


def _peer_route(h2, wq, keys):
    ntok, D = h2.shape
    T = ROUTE_TOKENS
    slots = PEER_HEADS * PEER_TOPK
    return pl.pallas_call(
        _route_kernel,
        grid=(ntok // T,),
        in_specs=[pl.BlockSpec((T, D), lambda i: (i, 0)), _const_spec(wq.shape), _const_spec(keys.shape)],
        out_specs=[pl.BlockSpec((slots, T), lambda i: (0, i)), pl.BlockSpec((slots, T), lambda i: (0, i))],
        out_shape=[jax.ShapeDtypeStruct((slots, ntok), jnp.int32), jax.ShapeDtypeStruct((slots, ntok), F32)],
        compiler_params=pltpu.CompilerParams(dimension_semantics=("parallel",),
                                             vmem_limit_bytes=VMEM_LIMIT_BYTES),
        name="peer_route",
    )(h2, wq, keys)


def _peer_kernel(eid_hbm, tab_hbm, x_ref, g_ref, o_ref, eid_s, *scratch):
    R = PEER_RING
    bufs, rsem, esem = scratch[:R], scratch[R], scratch[R + 1]
    i = pl.program_id(0)
    n = pl.num_programs(0)
    T, D = x_ref.shape
    S = g_ref.shape[-1]
    C = D // LANES
    blk = T * S
    ahead = R - 1

    def eid_copy(step):
        half = step % 2
        return pltpu.make_async_copy(eid_hbm.at[pl.ds(step * blk, blk)],
                                     eid_s.at[pl.ds(half * blk, blk)], esem.at[half])

    def issue(slot, first):
        for k in range(S):
            e = eid_s[first + k]
            pltpu.make_async_copy(tab_hbm.at[pl.ds(e * (2 * C), 2 * C)], bufs[slot].at[:, pl.ds(k, 1), :],
                                  rsem.at[slot]).start()

    def wait(slot):
        pltpu.make_async_copy(bufs[slot], bufs[slot], rsem.at[slot]).wait()

    cur = (i % 2) * blk
    nxt = ((i + 1) % 2) * blk

    @pl.when(i == 0)
    def _():
        eid_copy(0).start()
        eid_copy(0).wait()
        for t in range(ahead):
            issue(t, t * S)

    eid_copy(i + 1).start()

    eye = lax.broadcasted_iota(jnp.int32, (S, S), 0) == lax.broadcasted_iota(jnp.int32, (S, S), 1)

    def evaluate(slot, t):
        buf = bufs[slot]
        x_row = x_ref[pl.ds(t, 1), :]
        g_col = jnp.sum(jnp.where(eye, g_ref[pl.ds(t, 1), :], 0.0), axis=-1, keepdims=True)
        acc = buf[0] * x_row[:, 0:LANES]
        for s in range(1, C):
            acc = acc + buf[s] * x_row[:, s * LANES:(s + 1) * LANES]
        w = g_col * _gelu_tanh(jnp.sum(acc, axis=-1, keepdims=True))
        outs = [jnp.sum(w * buf[C + s], axis=0, keepdims=True) for s in range(C)]
        o_ref[pl.ds(t, 1), :] = jnp.concatenate(outs, axis=-1)

    def group(t0, last):
        for j in range(R):
            wait(j)
            look = j + ahead
            if last and look >= R:
                issue(look % R, nxt + (look - R) * S)
            else:
                issue(look % R, cur + (t0 + look) * S)
            evaluate(j, t0 + j)

    def body(gi, carry):
        group(pl.multiple_of(gi * R, R), False)
        return carry

    lax.fori_loop(0, T // R - 1, body, 0)
    eid_copy(i + 1).wait()
    group(T - R, True)

    @pl.when(i == n - 1)
    def _():
        for j in range(ahead):
            wait(j)


def _peer_experts(eid_flat, g, h2, tab):
    ntok, D = h2.shape
    S = g.shape[-1]
    T = PEER_TOK_BLOCK
    C = D // LANES
    assert T % PEER_RING == 0 and T > PEER_RING
    return pl.pallas_call(
        _peer_kernel,
        grid=(ntok // T,),
        in_specs=[pl.BlockSpec(memory_space=pl.ANY),
                  pl.BlockSpec(memory_space=pl.ANY),
                  pl.BlockSpec((T, D), lambda i: (i, 0)),
                  pl.BlockSpec((T, S), lambda i: (i, 0))],
        out_specs=pl.BlockSpec((T, D), lambda i: (i, 0)),
        out_shape=jax.ShapeDtypeStruct((ntok, D), F32),
        scratch_shapes=[pltpu.SMEM((2 * T * S,), jnp.int32)]
        + [pltpu.VMEM((2 * C, S, LANES), F32) for _ in range(PEER_RING)]
        + [pltpu.SemaphoreType.DMA((PEER_RING,)), pltpu.SemaphoreType.DMA((2,))],
        compiler_params=pltpu.CompilerParams(dimension_semantics=("arbitrary",),
                                             vmem_limit_bytes=VMEM_LIMIT_BYTES),
        name="peer_experts",
    )(eid_flat, tab, h2, g)


def _final_kernel(x1_ref, p_ref, gt_ref, gpost_ref, o_ref):
    o_ref[0] = x1_ref[0] + gt_ref[0] * (_rms(p_ref[0]) * gpost_ref[...])


def _final(x1, peer_out, gt2, g_post):
    B, L, D = x1.shape
    tm = min(2 * PROJ_TOKENS, L)
    tok = pl.BlockSpec((1, tm, D), lambda b, i: (b, i, 0))
    return pl.pallas_call(
        _final_kernel,
        grid=(B, L // tm),
        in_specs=[tok, tok, pl.BlockSpec((1, 1, D), lambda b, i: (b, 0, 0)), _const_spec((1, D))],
        out_specs=tok,
        out_shape=jax.ShapeDtypeStruct((B, L, D), F32),
        compiler_params=pltpu.CompilerParams(dimension_semantics=("parallel", "parallel")),
        name="final",
    )(x1, peer_out, gt2, g_post.reshape(1, D))


def _split_w_in(w_in, d_ssd, xbc_dim, nh2, d_conv):
    bounds = [0, d_ssd, d_ssd + xbc_dim, d_ssd + xbc_dim + nh2]
    bounds += [bounds[-1] + d_conv, bounds[-1] + 2 * d_conv, bounds[-1] + 3 * d_conv]
    wz, wxbc, wdt, wgb, wgc, wv = (w_in[:, bounds[i]:bounds[i + 1]].astype(MXU_DTYPE) for i in range(6))
    return wz, wxbc, wdt, wgb, wgc, wv


def kernel(x, c, ctx, c_ctx, w_ada, b_ada, g_pre_mix, g_post_mix, g_pre_ffn, g_post_ffn, w_in, ssd_conv_w, ssd_conv_b, ssd_dt_bias, ssd_a_log, ssd_d, ssd_norm_g, sc_conv_w, sc_norm_g, w_out, peer_w_q, peer_keys, peer_u, peer_v):
    B, L, D = x.shape
    ctx_len = ctx.shape[1]
    assert w_ada.shape[0] == 1, "single-layer block"
    nh2 = ssd_dt_bias.shape[1] * ssd_dt_bias.shape[2]
    nh = nh2 // 2
    d_ssd = nh * SSD_HEAD_DIM
    xbc_dim = ssd_conv_w.shape[-1]
    d_conv = sc_conv_w.shape[-1]
    n_mod = w_ada.shape[-1] // D

    pad_rows = (-(B + 1)) % 8
    cc = jnp.concatenate([c, c_ctx[None, :], jnp.zeros((pad_rows, D), F32)], axis=0)
    mod = _adaln(cc, w_ada[0], b_ada[0])
    mod_l = mod[:B].reshape(B, n_mod, 1, D)
    mod_c = jnp.broadcast_to(mod[B].reshape(1, n_mod, 1, D), (B, n_mod, 1, D))
    sh1_l, sc1_l, gt1_l, sh2_l, sc2_l, gt2_l = (mod_l[:, j] for j in range(n_mod))
    sh1_c, sc1_c = mod_c[:, 0], mod_c[:, 1]

    w = _split_w_in(w_in[0], d_ssd, xbc_dim, nh2, d_conv)
    proj_args = (w, ssd_conv_w[0], ssd_conv_b[0], ssd_dt_bias[0].reshape(nh2), sc_conv_w[0], sc_norm_g[0])
    a_log = ssd_a_log[0]

    _, xbc_c, dt_c, dtT_c, _ = _in_proj(ctx, sh1_c, sc1_c, g_pre_mix[0], *proj_args, row_w=ctx_len)
    gn = SSD_GROUPS * SSD_STATE
    h0 = jnp.zeros((B, 2, gn, d_ssd // SSD_GROUPS), F32)
    _, h_ctx = _ssd_scan(xbc_c, dt_c, dtT_c, a_log, h0)

    z_l, xbc_l, dt_l, dtT_l, sco_l = _in_proj(x, sh1_l, sc1_l, g_pre_mix[0], *proj_args, row_w=GRID_W)
    y2, _ = _ssd_scan(xbc_l, dt_l, dtT_l, a_log, h_ctx)

    wo = w_out[0].astype(MXU_DTYPE)
    d_skip_e = jnp.repeat(ssd_d[0], SSD_HEAD_DIM)
    x1, h2 = _mix_out(x, y2, xbc_l, z_l, sco_l, d_skip_e, ssd_norm_g[0], wo[:d_ssd], wo[d_ssd:],
                      g_post_mix[0], gt1_l, g_pre_ffn[0], sh2_l, sc2_l)

    ntok = B * L
    qd = peer_w_q.shape[-1] // PEER_HEADS
    wq = peer_w_q[0].reshape(D, PEER_HEADS, qd).transpose(1, 0, 2).astype(MXU_DTYPE)
    eidT, gT = _peer_route(h2.reshape(ntok, D), wq, peer_keys[0].astype(MXU_DTYPE))
    tab = jnp.concatenate([peer_u[0], peer_v[0]], axis=1).reshape(-1, 1, LANES)
    eid = jnp.pad(eidT.T.reshape(-1), (0, PEER_TOK_BLOCK * eidT.shape[0]))
    peer_out = _peer_experts(eid, gT.T, h2.reshape(ntok, D), tab)

    return _final(x1, peer_out.reshape(B, L, D), gt2_l, g_post_ffn[0])
```

```python
import functools

import jax
import jax.numpy as jnp
import numpy as np
from jax import lax
from jax.experimental import pallas as pl
from jax.experimental.pallas import tpu as pltpu

GRID_W = 64
SSD_HEAD_DIM = 64
SSD_GROUPS = 4
SSD_STATE = 128
SSD_CHUNK = 128
PEER_HEADS = 8
PEER_N_KEYS = 128
PEER_TOPK = 16
EPS = 1e-6
LANES = 128

MXU_DTYPE = jnp.bfloat16
VMEM_LIMIT_BYTES = 56 * 1024 * 1024

PROJ_TOKENS = 256
ROUTE_TOKENS = 1024
PEER_TOK_BLOCK = 128
PEER_RING = 8

F32 = jnp.float32


def _dot(a, b):
    return jnp.dot(a, b, preferred_element_type=F32)


def _dot_nt(a, b):
    return lax.dot_general(a, b, (((1,), (1,)), ((), ())), preferred_element_type=F32)


def _dot_tn(a, b):
    return lax.dot_general(a, b, (((0,), (0,)), ((), ())), preferred_element_type=F32)


def _split_bf16(a, terms):
    parts = []
    r = a
    for i in range(terms):
        p = r.astype(MXU_DTYPE)
        parts.append(p)
        if i + 1 < terms:
            r = r - p.astype(F32)
    return parts


def _silu(x):
    return x * jax.nn.sigmoid(x)


def _softplus(x):
    return jnp.maximum(x, 0.0) + jnp.log1p(jnp.exp(-jnp.abs(x)))


def _gelu_tanh(x):
    return 0.5 * x * (1.0 + jnp.tanh(0.7978845608028654 * (x + 0.044715 * (x * x * x))))


def _rms(x):
    return x * lax.rsqrt(jnp.mean(x * x, axis=-1, keepdims=True) + EPS)


def _const_spec(shape):
    zeros = (0,) * len(shape)
    return pl.BlockSpec(shape, lambda *_: zeros, pipeline_mode=pl.Buffered(1))


def _adaln_kernel(c_ref, w_ref, b_ref, o_ref):
    s = _silu(c_ref[...]).astype(MXU_DTYPE)
    o_ref[...] = _dot(s, w_ref[...].astype(MXU_DTYPE)) + b_ref[...]


def _adaln(cc, w, b):
    rows, d = cc.shape
    n = w.shape[1]
    bn = 1024
    return pl.pallas_call(
        _adaln_kernel,
        grid=(n // bn,),
        in_specs=[pl.BlockSpec((rows, d), lambda j: (0, 0)),
                  pl.BlockSpec((d, bn), lambda j: (0, j)),
                  pl.BlockSpec((1, bn), lambda j: (0, j))],
        out_specs=pl.BlockSpec((rows, bn), lambda j: (0, j)),
        out_shape=jax.ShapeDtypeStruct((rows, n), F32),
        name="adaln",
    )(cc, w, b.reshape(1, n))


def _dwconv(u, w, pos, row_w):
    taps = w.shape[0]
    p = taps // 2
    n = u.shape[0]
    out = u * w[p:p + 1, :]
    for k in range(taps):
        d = k - p
        if d == 0:
            continue
        shifted = pltpu.roll(u, (-d) % n, axis=0)
        valid = jnp.logical_and(pos + d >= 0, pos + d < row_w)
        out = out + jnp.where(valid, shifted, 0.0) * w[k:k + 1, :]
    return out


def _in_proj_kernel(x_ref, sh_ref, sc_ref, gpre_ref, wz_ref, wxbc_ref, wdt_ref, wdtT_ref, wgb_ref, wgc_ref,
                    wv_ref, cw_ref, cb_ref, dtb_ref, dtbT_ref, scw_ref, scg_ref,
                    z_ref, xbc_ref, dt_ref, dtT_ref, sco_ref, *, row_w):
    x = x_ref[0]
    tm = x.shape[0]
    h = _rms(x) * gpre_ref[...]
    h = h * (1.0 + sc_ref[0]) + sh_ref[0]
    hb = h.astype(MXU_DTYPE)
    pos = lax.broadcasted_iota(jnp.int32, (tm, 1), 0) % row_w

    z_ref[0] = _dot(hb, wz_ref[...])
    xbc = _dwconv(_dot(hb, wxbc_ref[...]), cw_ref[...], pos, row_w) + cb_ref[...]
    xbc_ref[0] = _silu(xbc)

    nh = dt_ref.shape[-1]
    dt = _softplus(_dot(hb, wdt_ref[...]) + dtb_ref[...])
    dt_ref[0, 0] = dt[:, :nh]
    dt_ref[1, 0] = dt[:, nh:]
    dtT = _softplus(_dot_nt(wdtT_ref[...], hb) + dtbT_ref[...])
    dtT_ref[0, 0] = dtT[:nh]
    dtT_ref[1, 0] = dtT[nh:]

    gcv = _dot(hb, wgc_ref[...]) * _dot(hb, wv_ref[...])
    sc = _dot(hb, wgb_ref[...]) * _dwconv(gcv, scw_ref[...], pos, row_w)
    sco_ref[0] = _rms(sc) * scg_ref[...]


def _in_proj(x, sh, sc, g_pre, w, conv_w, conv_b, dt_bias, sc_conv_w, sc_norm_g, row_w):
    B, L, D = x.shape
    tm = min(PROJ_TOKENS, L)
    assert L % tm == 0 and tm % row_w == 0
    wz, wxbc, wdt, wgb, wgc, wv = w
    d_ssd, xbc_dim, nh2, d_conv = wz.shape[1], wxbc.shape[1], wdt.shape[1], wgb.shape[1]
    nh = nh2 // 2
    tok = lambda cols: pl.BlockSpec((1, tm, cols), lambda b, i: (b, i, 0))
    per_b = pl.BlockSpec((1, 1, D), lambda b, i: (b, 0, 0))
    return pl.pallas_call(
        functools.partial(_in_proj_kernel, row_w=row_w),
        grid=(B, L // tm),
        in_specs=[tok(D), per_b, per_b, _const_spec((1, D)),
                  _const_spec(wz.shape), _const_spec(wxbc.shape), _const_spec(wdt.shape),
                  _const_spec((nh2, D)), _const_spec(wgb.shape), _const_spec(wgc.shape), _const_spec(wv.shape),
                  _const_spec(conv_w.shape), _const_spec((1, xbc_dim)), _const_spec((1, nh2)),
                  _const_spec((nh2, 1)), _const_spec(sc_conv_w.shape), _const_spec((1, d_conv))],
        out_specs=[tok(d_ssd), tok(xbc_dim),
                   pl.BlockSpec((2, 1, tm, nh), lambda b, i: (0, b, i, 0)),
                   pl.BlockSpec((2, 1, nh, tm), lambda b, i: (0, b, 0, i)),
                   tok(d_conv)],
        out_shape=[jax.ShapeDtypeStruct((B, L, d_ssd), F32),
                   jax.ShapeDtypeStruct((B, L, xbc_dim), F32),
                   jax.ShapeDtypeStruct((2, B, L, nh), F32),
                   jax.ShapeDtypeStruct((2, B, nh, L), F32),
                   jax.ShapeDtypeStruct((B, L, d_conv), F32)],
        compiler_params=pltpu.CompilerParams(dimension_semantics=("parallel", "parallel"),
                                             vmem_limit_bytes=VMEM_LIMIT_BYTES),
        name="in_proj",
    )(x, sh, sc, g_pre.reshape(1, D), wz, wxbc, wdt, wdt.T, wgb, wgc, wv,
      conv_w, conv_b.reshape(1, xbc_dim), dt_bias.reshape(1, nh2), dt_bias.reshape(nh2, 1),
      sc_conv_w, sc_norm_g.reshape(1, d_conv))


def _ssd_kernel(xs_ref, bm_ref, cm_ref, dt_ref, dtT_ref, alog_ref, alogT_ref, h0_ref, y_ref, hf_ref, st_ref):
    d = pl.program_id(1)
    c = pl.program_id(2)
    nc = pl.num_programs(2)
    Q = SSD_CHUNK
    N = SSD_STATE
    P = SSD_HEAD_DIM
    d_ssd = xs_ref.shape[-1]
    nh = dt_ref.shape[-1]
    hpg = nh // SSD_GROUPS
    gw = hpg * P

    @pl.when(c == 0)
    def _():
        st_ref[...] = h0_ref[0, 0]

    a = dt_ref[0, 0] * (-jnp.exp(alog_ref[0]))
    aT = dtT_ref[0, 0] * (-jnp.exp(alogT_ref[0]))
    row = lax.broadcasted_iota(jnp.int32, (Q, Q), 0)
    col = lax.broadcasted_iota(jnp.int32, (Q, Q), 1)
    fwd = d == 0
    mask = (row - col) * (1 - 2 * d) >= 0
    maskb = mask.astype(F32).astype(MXU_DTYPE)
    acs = sum(_dot(maskb, p) for p in _split_bf16(a, 3))
    acsT = sum(_dot_nt(p, maskb) for p in _split_bf16(aT, 3))

    erow = lax.broadcasted_iota(jnp.int32, (nh, d_ssd), 0)
    ecol = lax.broadcasted_iota(jnp.int32, (nh, d_ssd), 1)
    expand = (ecol // P == erow).astype(F32).astype(MXU_DTYPE)
    acs_e = sum(_dot(p, expand) for p in _split_bf16(acs, 2))
    dt_e = sum(_dot(p, expand) for p in _split_bf16(dt_ref[0, 0], 2))
    tot_e = jnp.where(fwd, acs_e[Q - 1:Q, :], acs_e[0:1, :])

    xdt = xs_ref[0] * dt_e
    xw = (xdt * jnp.exp(tot_e - acs_e)).astype(MXU_DTYPE)
    xdtb = xdt.astype(MXU_DTYPE)
    eacs = jnp.exp(acs_e)
    cdec = jnp.exp(tot_e)
    lane = lax.broadcasted_iota(jnp.int32, (Q, 2 * P), 1)

    for g in range(SSD_GROUPS):
        bm = bm_ref[0, :, g * N:(g + 1) * N].astype(MXU_DTYPE)
        cm = cm_ref[0, :, g * N:(g + 1) * N].astype(MXU_DTYPE)
        cb = _dot_nt(cm, bm)
        hT = st_ref[g * N:(g + 1) * N, :]
        gcols = slice(g * gw, (g + 1) * gw)
        y_off = _dot(cm, hT.astype(MXU_DTYPE)) * eacs[:, gcols]
        st_ref[g * N:(g + 1) * N, :] = hT * cdec[:, gcols] + _dot_tn(bm, xw[:, gcols])
        for pr in range(hpg // 2):
            pcols = slice(g * gw + pr * 2 * P, g * gw + (pr + 1) * 2 * P)
            xpair = xdtb[:, pcols]
            halves = []
            for r in range(2):
                hd = g * hpg + pr * 2 + r
                seg = acs[:, hd:hd + 1] - acsT[hd:hd + 1, :]
                decay = jnp.exp(jnp.where(mask, seg, -jnp.inf))
                halves.append(_dot((cb * decay).astype(MXU_DTYPE), xpair))
            y_diag = jnp.where(lane < P, halves[0], halves[1])
            y_ref[0, 0, :, pcols] = y_diag + y_off[:, pr * 2 * P:(pr + 1) * 2 * P]

    @pl.when(c == nc - 1)
    def _():
        hf_ref[0, 0] = st_ref[...]


def _ssd_scan(xbc, dt, dtT, a_log, h0):
    B, L, _ = xbc.shape
    nh = dt.shape[-1]
    d_ssd = nh * SSD_HEAD_DIM
    gn = SSD_GROUPS * SSD_STATE
    Q = SSD_CHUNK
    nc = L // Q
    gw = d_ssd // SSD_GROUPS
    chunk = lambda d, c: c + d * (nc - 1 - 2 * c)
    return pl.pallas_call(
        _ssd_kernel,
        grid=(B, 2, nc),
        in_specs=[pl.BlockSpec((1, Q, d_ssd), lambda b, d, c: (b, chunk(d, c), 0)),
                  pl.BlockSpec((1, Q, gn), lambda b, d, c: (b, chunk(d, c), d_ssd // gn)),
                  pl.BlockSpec((1, Q, gn), lambda b, d, c: (b, chunk(d, c), d_ssd // gn + 1)),
                  pl.BlockSpec((1, 1, Q, nh), lambda b, d, c: (d, b, chunk(d, c), 0)),
                  pl.BlockSpec((1, 1, nh, Q), lambda b, d, c: (d, b, 0, chunk(d, c))),
                  pl.BlockSpec((1, 1, nh), lambda b, d, c: (d, 0, 0)),
                  pl.BlockSpec((1, nh, 1), lambda b, d, c: (d, 0, 0)),
                  pl.BlockSpec((1, 1, gn, gw), lambda b, d, c: (b, d, 0, 0))],
        out_specs=[pl.BlockSpec((1, 1, Q, d_ssd), lambda b, d, c: (d, b, chunk(d, c), 0)),
                   pl.BlockSpec((1, 1, gn, gw), lambda b, d, c: (b, d, 0, 0))],
        out_shape=[jax.ShapeDtypeStruct((2, B, L, d_ssd), F32),
                   jax.ShapeDtypeStruct((B, 2, gn, gw), F32)],
        scratch_shapes=[pltpu.VMEM((gn, gw), F32)],
        compiler_params=pltpu.CompilerParams(dimension_semantics=("parallel", "arbitrary", "arbitrary"),
                                             vmem_limit_bytes=VMEM_LIMIT_BYTES),
        name="ssd_scan",
    )(xbc, xbc, xbc, dt, dtT, a_log.reshape(2, 1, nh), a_log.reshape(2, nh, 1), h0)


def _mix_out_kernel(x_ref, yf_ref, yb_ref, xs_ref, z_ref, sco_ref, dsk_ref, ng_ref, wo1_ref, wo2_ref,
                    gpost_ref, gt_ref, gpre_ref, sh_ref, sc_ref, x1_ref, h2_ref):
    d_ssd = xs_ref.shape[-1]
    gw = d_ssd // SSD_GROUPS
    y = yf_ref[0, 0] + yb_ref[0, 0] + dsk_ref[...] * xs_ref[0]
    y = y * _silu(z_ref[0])
    parts = [_rms(y[:, g * gw:(g + 1) * gw]) for g in range(SSD_GROUPS)]
    yn = jnp.concatenate(parts, axis=-1) * ng_ref[...]
    mix = _dot(yn.astype(MXU_DTYPE), wo1_ref[...]) + _dot(sco_ref[0].astype(MXU_DTYPE), wo2_ref[...])
    x1 = x_ref[0] + gt_ref[0] * (_rms(mix) * gpost_ref[...])
    x1_ref[0] = x1
    h2 = _rms(x1) * gpre_ref[...]
    h2_ref[0] = h2 * (1.0 + sc_ref[0]) + sh_ref[0]


def _mix_out(x, y2, xbc, z, sco, d_skip_e, norm_g, wo1, wo2, g_post, gt1, g_pre_ffn, sh2, sc2):
    B, L, D = x.shape
    d_ssd = z.shape[-1]
    d_conv = sco.shape[-1]
    tm = min(PROJ_TOKENS, L)
    tok = lambda cols: pl.BlockSpec((1, tm, cols), lambda b, i: (b, i, 0))
    per_b = pl.BlockSpec((1, 1, D), lambda b, i: (b, 0, 0))
    ydir = lambda d: pl.BlockSpec((1, 1, tm, d_ssd), lambda b, i: (d, b, i, 0))
    return pl.pallas_call(
        _mix_out_kernel,
        grid=(B, L // tm),
        in_specs=[tok(D), ydir(0), ydir(1), tok(d_ssd), tok(d_ssd), tok(d_conv),
                  _const_spec((1, d_ssd)), _const_spec((1, d_ssd)), _const_spec(wo1.shape), _const_spec(wo2.shape),
                  _const_spec((1, D)), per_b, _const_spec((1, D)), per_b, per_b],
        out_specs=[tok(D), tok(D)],
        out_shape=[jax.ShapeDtypeStruct((B, L, D), F32), jax.ShapeDtypeStruct((B, L, D), F32)],
        compiler_params=pltpu.CompilerParams(dimension_semantics=("parallel", "parallel"),
                                             vmem_limit_bytes=VMEM_LIMIT_BYTES),
        name="mix_out",
    )(x, y2, y2, xbc, z, sco, d_skip_e.reshape(1, d_ssd), norm_g.reshape(1, d_ssd), wo1, wo2,
      g_post.reshape(1, D), gt1, g_pre_ffn.reshape(1, D), sh2, sc2)


def _topk_rows(s, k):
    n = s.shape[0]
    rows = lax.broadcasted_iota(jnp.int32, s.shape, 0).astype(F32)
    vals, idxs = [], []
    for _ in range(k):
        m = jnp.max(s, axis=0, keepdims=True)
        i = jnp.min(jnp.where(s == m, rows, float(n)), axis=0, keepdims=True)
        vals.append(m)
        idxs.append(i)
        s = jnp.where(rows == i, -jnp.inf, s)
    return jnp.concatenate(vals, axis=0), jnp.concatenate(idxs, axis=0)


def _pair_candidates(k):
    return [(i, j) for i in range(k) for j in range(k) if (i + 1) * (j + 1) <= k]


def _pair_select_matrices(k):
    pairs = _pair_candidates(k)
    rows = -(-len(pairs) // 8) * 8
    sel0 = np.zeros((rows, k), np.float32)
    sel1 = np.zeros((rows, k), np.float32)
    bias = np.full((rows, 1), -np.inf, np.float32)
    for r, (i, j) in enumerate(pairs):
        sel0[r, i] = 1.0
        sel1[r, j] = 1.0
        bias[r, 0] = 0.0
    return sel0, sel1, bias


def _select_rows(sel, x):
    p = _split_bf16(x, 3)
    return (_dot(sel, p[0]) + _dot(sel, p[1])) + _dot(sel, p[2])


def _route_kernel(h2_ref, wq_ref, keys_ref, sel0_ref, sel1_ref, bias_ref, eid_ref, g_ref):
    K = PEER_TOPK
    NK = PEER_N_KEYS
    hb = h2_ref[...].astype(MXU_DTYPE)
    T = hb.shape[0]
    half = keys_ref.shape[-1]
    sel0 = sel0_ref[...]
    sel1 = sel1_ref[...]
    n_cand = sel0.shape[0]
    c_rows = lax.broadcasted_iota(jnp.int32, (n_cand, T), 0).astype(F32)

    def head(h, carry):
        q = _dot(hb, wq_ref[h]).astype(MXU_DTYPE)
        sv, si = [], []
        for p in range(2):
            s = _dot_nt(keys_ref[h, p], q[:, p * half:(p + 1) * half])
            v, i = _topk_rows(s, K)
            sv.append(v)
            si.append(i)
        cand = _select_rows(sel0, sv[0]) + _select_rows(sel1, sv[1]) + bias_ref[...]
        ecand = _dot(sel0, si[0].astype(MXU_DTYPE)) * float(NK) + _dot(sel1, si[1].astype(MXU_DTYPE))
        cv, eid = [], []
        for _ in range(K):
            m = jnp.max(cand, axis=0, keepdims=True)
            i = jnp.min(jnp.where(cand == m, c_rows, float(n_cand)), axis=0, keepdims=True)
            hit = c_rows == i
            cv.append(m)
            eid.append(jnp.max(jnp.where(hit, ecand, -1.0), axis=0, keepdims=True))
            cand = jnp.where(hit, -jnp.inf, cand)
        cvs = jnp.concatenate(cv, axis=0)
        ex = jnp.exp(cvs - cv[0])
        gate = ex / jnp.sum(ex, axis=0, keepdims=True)
        rows = pl.ds(pl.multiple_of(h * K, K), K)
        eid_ref[rows, :] = jnp.concatenate(eid, axis=0).astype(jnp.int32)
        g_ref[rows, :] = gate
        return carry

    lax.fori_loop(0, PEER_HEADS, head, 0)


def _peer_route(h2, wq, keys):
    ntok, D = h2.shape
    T = ROUTE_TOKENS
    slots = PEER_HEADS * PEER_TOPK
    sel0, sel1, bias = _pair_select_matrices(PEER_TOPK)
    sel0 = jnp.asarray(sel0, MXU_DTYPE)
    sel1 = jnp.asarray(sel1, MXU_DTYPE)
    return pl.pallas_call(
        _route_kernel,
        grid=(ntok // T,),
        in_specs=[pl.BlockSpec((T, D), lambda i: (i, 0)), _const_spec(wq.shape), _const_spec(keys.shape),
                  _const_spec(sel0.shape), _const_spec(sel1.shape), _const_spec(bias.shape)],
        out_specs=[pl.BlockSpec((slots, T), lambda i: (0, i)), pl.BlockSpec((slots, T), lambda i: (0, i))],
        out_shape=[jax.ShapeDtypeStruct((slots, ntok), jnp.int32), jax.ShapeDtypeStruct((slots, ntok), F32)],
        compiler_params=pltpu.CompilerParams(dimension_semantics=("parallel",),
                                             vmem_limit_bytes=VMEM_LIMIT_BYTES),
        name="peer_route",
    )(h2, wq, keys, sel0, sel1, jnp.asarray(bias))


def _peer_kernel(eid_hbm, tab_hbm, x_ref, g_ref, o_ref, eid_s, *scratch):
    R = PEER_RING
    bufs, rsem, esem = scratch[:R], scratch[R], scratch[R + 1]
    i = pl.program_id(0)
    n = pl.num_programs(0)
    T, D = x_ref.shape
    S = g_ref.shape[-1]
    C = D // LANES
    blk = T * S
    ahead = R - 1

    def eid_copy(step):
        half = step % 2
        return pltpu.make_async_copy(eid_hbm.at[pl.ds(step * blk, blk)],
                                     eid_s.at[pl.ds(half * blk, blk)], esem.at[half])

    def issue(slot, first):
        for k in range(S):
            e = eid_s[first + k]
            pltpu.async_copy(tab_hbm.at[pl.ds(e * (2 * C), 2 * C), :], bufs[slot].at[:, k, :], rsem.at[slot],
                             priority=k % 2)

    def wait(slot):
        pltpu.make_async_copy(bufs[slot], bufs[slot], rsem.at[slot]).wait()

    cur = (i % 2) * blk
    nxt = ((i + 1) % 2) * blk

    @pl.when(i == 0)
    def _():
        eid_copy(0).start()
        eid_copy(0).wait()
        for t in range(ahead):
            issue(t, t * S)

    eid_copy(i + 1).start()

    eye = lax.broadcasted_iota(jnp.int32, (S, S), 0) == lax.broadcasted_iota(jnp.int32, (S, S), 1)

    def evaluate(slot, t):
        buf = bufs[slot]
        x_row = x_ref[pl.ds(t, 1), :]
        g_col = jnp.sum(jnp.where(eye, g_ref[pl.ds(t, 1), :], 0.0), axis=-1, keepdims=True)
        acc = buf[0] * x_row[:, 0:LANES]
        for s in range(1, C):
            acc = acc + buf[s] * x_row[:, s * LANES:(s + 1) * LANES]
        w = g_col * _gelu_tanh(jnp.sum(acc, axis=-1, keepdims=True))
        outs = [jnp.sum(w * buf[C + s], axis=0, keepdims=True) for s in range(C)]
        o_ref[pl.ds(t, 1), :] = jnp.concatenate(outs, axis=-1)

    def group(t0, last):
        for j in range(R):
            wait(j)
            look = j + ahead
            if last and look >= R:
                issue(look % R, nxt + (look - R) * S)
            else:
                issue(look % R, cur + (t0 + look) * S)
            evaluate(j, t0 + j)

    def body(gi, carry):
        group(pl.multiple_of(gi * R, R), False)
        return carry

    lax.fori_loop(0, T // R - 1, body, 0)
    eid_copy(i + 1).wait()
    group(T - R, True)

    @pl.when(i == n - 1)
    def _():
        for j in range(ahead):
            wait(j)


def _peer_experts(eid_flat, g, h2, tab):
    ntok, D = h2.shape
    S = g.shape[-1]
    T = PEER_TOK_BLOCK
    C = D // LANES
    assert T % PEER_RING == 0 and T > PEER_RING
    return pl.pallas_call(
        _peer_kernel,
        grid=(ntok // T,),
        in_specs=[pl.BlockSpec(memory_space=pl.ANY),
                  pl.BlockSpec(memory_space=pl.ANY),
                  pl.BlockSpec((T, D), lambda i: (i, 0)),
                  pl.BlockSpec((T, S), lambda i: (i, 0))],
        out_specs=pl.BlockSpec((T, D), lambda i: (i, 0)),
        out_shape=jax.ShapeDtypeStruct((ntok, D), F32),
        scratch_shapes=[pltpu.SMEM((2 * T * S,), jnp.int32)]
        + [pltpu.VMEM((2 * C, S, LANES), F32) for _ in range(PEER_RING)]
        + [pltpu.SemaphoreType.DMA((PEER_RING,)), pltpu.SemaphoreType.DMA((2,))],
        compiler_params=pltpu.CompilerParams(dimension_semantics=("arbitrary",),
                                             vmem_limit_bytes=VMEM_LIMIT_BYTES),
        name="peer_experts",
    )(eid_flat, tab, h2, g)


def _final_kernel(x1_ref, p_ref, gt_ref, gpost_ref, o_ref):
    o_ref[0] = x1_ref[0] + gt_ref[0] * (_rms(p_ref[0]) * gpost_ref[...])


def _final(x1, peer_out, gt2, g_post):
    B, L, D = x1.shape
    tm = min(2 * PROJ_TOKENS, L)
    tok = pl.BlockSpec((1, tm, D), lambda b, i: (b, i, 0))
    return pl.pallas_call(
        _final_kernel,
        grid=(B, L // tm),
        in_specs=[tok, tok, pl.BlockSpec((1, 1, D), lambda b, i: (b, 0, 0)), _const_spec((1, D))],
        out_specs=tok,
        out_shape=jax.ShapeDtypeStruct((B, L, D), F32),
        compiler_params=pltpu.CompilerParams(dimension_semantics=("parallel", "parallel")),
        name="final",
    )(x1, peer_out, gt2, g_post.reshape(1, D))


def _split_w_in(w_in, d_ssd, xbc_dim, nh2, d_conv):
    bounds = [0, d_ssd, d_ssd + xbc_dim, d_ssd + xbc_dim + nh2]
    bounds += [bounds[-1] + d_conv, bounds[-1] + 2 * d_conv, bounds[-1] + 3 * d_conv]
    wz, wxbc, wdt, wgb, wgc, wv = (w_in[:, bounds[i]:bounds[i + 1]].astype(MXU_DTYPE) for i in range(6))
    return wz, wxbc, wdt, wgb, wgc, wv


def kernel(x, c, ctx, c_ctx, w_ada, b_ada, g_pre_mix, g_post_mix, g_pre_ffn, g_post_ffn, w_in, ssd_conv_w, ssd_conv_b, ssd_dt_bias, ssd_a_log, ssd_d, ssd_norm_g, sc_conv_w, sc_norm_g, w_out, peer_w_q, peer_keys, peer_u, peer_v):
    B, L, D = x.shape
    ctx_len = ctx.shape[1]
    assert w_ada.shape[0] == 1, "single-layer block"
    nh2 = ssd_dt_bias.shape[1] * ssd_dt_bias.shape[2]
    nh = nh2 // 2
    d_ssd = nh * SSD_HEAD_DIM
    xbc_dim = ssd_conv_w.shape[-1]
    d_conv = sc_conv_w.shape[-1]
    n_mod = w_ada.shape[-1] // D

    pad_rows = (-(B + 1)) % 8
    cc = jnp.concatenate([c, c_ctx[None, :], jnp.zeros((pad_rows, D), F32)], axis=0)
    mod = _adaln(cc, w_ada[0], b_ada[0])
    mod_l = mod[:B].reshape(B, n_mod, 1, D)
    mod_c = jnp.broadcast_to(mod[B].reshape(1, n_mod, 1, D), (B, n_mod, 1, D))
    sh1_l, sc1_l, gt1_l, sh2_l, sc2_l, gt2_l = (mod_l[:, j] for j in range(n_mod))
    sh1_c, sc1_c = mod_c[:, 0], mod_c[:, 1]

    w = _split_w_in(w_in[0], d_ssd, xbc_dim, nh2, d_conv)
    proj_args = (w, ssd_conv_w[0], ssd_conv_b[0], ssd_dt_bias[0].reshape(nh2), sc_conv_w[0], sc_norm_g[0])
    a_log = ssd_a_log[0]

    _, xbc_c, dt_c, dtT_c, _ = _in_proj(ctx, sh1_c, sc1_c, g_pre_mix[0], *proj_args, row_w=ctx_len)
    gn = SSD_GROUPS * SSD_STATE
    h0 = jnp.zeros((B, 2, gn, d_ssd // SSD_GROUPS), F32)
    _, h_ctx = _ssd_scan(xbc_c, dt_c, dtT_c, a_log, h0)

    z_l, xbc_l, dt_l, dtT_l, sco_l = _in_proj(x, sh1_l, sc1_l, g_pre_mix[0], *proj_args, row_w=GRID_W)
    y2, _ = _ssd_scan(xbc_l, dt_l, dtT_l, a_log, h_ctx)

    wo = w_out[0].astype(MXU_DTYPE)
    d_skip_e = jnp.repeat(ssd_d[0], SSD_HEAD_DIM)
    x1, h2 = _mix_out(x, y2, xbc_l, z_l, sco_l, d_skip_e, ssd_norm_g[0], wo[:d_ssd], wo[d_ssd:],
                      g_post_mix[0], gt1_l, g_pre_ffn[0], sh2_l, sc2_l)

    ntok = B * L
    qd = peer_w_q.shape[-1] // PEER_HEADS
    wq = peer_w_q[0].reshape(D, PEER_HEADS, qd).transpose(1, 0, 2).astype(MXU_DTYPE)
    eidT, gT = _peer_route(h2.reshape(ntok, D), wq, peer_keys[0].astype(MXU_DTYPE))
    n_exp = peer_u.shape[1]
    tab = jnp.concatenate([peer_u[0].reshape(n_exp, -1, LANES), peer_v[0].reshape(n_exp, -1, LANES)],
                          axis=1).reshape(-1, LANES)
    eid = jnp.pad(eidT.T.reshape(-1), (0, PEER_TOK_BLOCK * eidT.shape[0]))
    peer_out = _peer_experts(eid, gT.T, h2.reshape(ntok, D), tab)

    return _final(x1, peer_out.reshape(B, L, D), gt2_l, g_post_ffn[0])
```

```python
import functools

import jax
import jax.numpy as jnp
import numpy as np
from jax import lax
from jax.experimental import pallas as pl
from jax.experimental.pallas import tpu as pltpu

GRID_W = 64
SSD_HEAD_DIM = 64
SSD_GROUPS = 4
SSD_STATE = 128
SSD_CHUNK = 128
PEER_HEADS = 8
PEER_N_KEYS = 128
PEER_TOPK = 16
EPS = 1e-6
LANES = 128

MXU_DTYPE = jnp.bfloat16
VMEM_LIMIT_BYTES = 56 * 1024 * 1024

PROJ_TOKENS = 256
ROUTE_TOKENS = 1024
PEER_TOK_BLOCK = 128
PEER_RING = 8

F32 = jnp.float32


def _dot(a, b):
    return jnp.dot(a, b, preferred_element_type=F32)


def _dot_nt(a, b):
    return lax.dot_general(a, b, (((1,), (1,)), ((), ())), preferred_element_type=F32)


def _dot_tn(a, b):
    return lax.dot_general(a, b, (((0,), (0,)), ((), ())), preferred_element_type=F32)


def _split_bf16(a, terms):
    parts = []
    r = a
    for i in range(terms):
        p = r.astype(MXU_DTYPE)
        parts.append(p)
        if i + 1 < terms:
            r = r - p.astype(F32)
    return parts


def _silu(x):
    return x * jax.nn.sigmoid(x)


def _softplus(x):
    return jnp.maximum(x, 0.0) + jnp.log1p(jnp.exp(-jnp.abs(x)))


def _gelu_tanh(x):
    return 0.5 * x * (1.0 + jnp.tanh(0.7978845608028654 * (x + 0.044715 * (x * x * x))))


def _rms(x):
    return x * lax.rsqrt(jnp.mean(x * x, axis=-1, keepdims=True) + EPS)


def _const_spec(shape):
    zeros = (0,) * len(shape)
    return pl.BlockSpec(shape, lambda *_: zeros, pipeline_mode=pl.Buffered(1))


def _adaln_kernel(c_ref, w_ref, b_ref, o_ref):
    s = _silu(c_ref[...]).astype(MXU_DTYPE)
    o_ref[...] = _dot(s, w_ref[...].astype(MXU_DTYPE)) + b_ref[...]


def _adaln(cc, w, b):
    rows, d = cc.shape
    n = w.shape[1]
    bn = 1024
    return pl.pallas_call(
        _adaln_kernel,
        grid=(n // bn,),
        in_specs=[pl.BlockSpec((rows, d), lambda j: (0, 0)),
                  pl.BlockSpec((d, bn), lambda j: (0, j)),
                  pl.BlockSpec((1, bn), lambda j: (0, j))],
        out_specs=pl.BlockSpec((rows, bn), lambda j: (0, j)),
        out_shape=jax.ShapeDtypeStruct((rows, n), F32),
        name="adaln",
    )(cc, w, b.reshape(1, n))


def _dwconv(u, w, pos, row_w):
    taps = w.shape[0]
    p = taps // 2
    n = u.shape[0]
    out = u * w[p:p + 1, :]
    for k in range(taps):
        d = k - p
        if d == 0:
            continue
        shifted = pltpu.roll(u, (-d) % n, axis=0)
        valid = jnp.logical_and(pos + d >= 0, pos + d < row_w)
        out = out + jnp.where(valid, shifted, 0.0) * w[k:k + 1, :]
    return out


def _in_proj_kernel(x_ref, sh_ref, sc_ref, gpre_ref, wz_ref, wxbc_ref, wdt_ref, wdtT_ref, wgb_ref, wgc_ref,
                    wv_ref, cw_ref, cb_ref, dtb_ref, dtbT_ref, scw_ref, scg_ref,
                    z_ref, xbc_ref, dt_ref, dtT_ref, sco_ref, *, row_w):
    x = x_ref[0]
    tm = x.shape[0]
    h = _rms(x) * gpre_ref[...]
    h = h * (1.0 + sc_ref[0]) + sh_ref[0]
    hb = h.astype(MXU_DTYPE)
    pos = lax.broadcasted_iota(jnp.int32, (tm, 1), 0) % row_w

    z_ref[0] = _dot(hb, wz_ref[...])
    xbc = _dwconv(_dot(hb, wxbc_ref[...]), cw_ref[...], pos, row_w) + cb_ref[...]
    xbc_ref[0] = _silu(xbc)

    nh = dt_ref.shape[-1]
    dt = _softplus(_dot(hb, wdt_ref[...]) + dtb_ref[...])
    dt_ref[0, 0] = dt[:, :nh]
    dt_ref[1, 0] = dt[:, nh:]
    dtT = _softplus(_dot_nt(wdtT_ref[...], hb) + dtbT_ref[...])
    dtT_ref[0, 0] = dtT[:nh]
    dtT_ref[1, 0] = dtT[nh:]

    gcv = _dot(hb, wgc_ref[...]) * _dot(hb, wv_ref[...])
    sc = _dot(hb, wgb_ref[...]) * _dwconv(gcv, scw_ref[...], pos, row_w)
    sco_ref[0] = _rms(sc) * scg_ref[...]


def _in_proj(x, sh, sc, g_pre, w, conv_w, conv_b, dt_bias, sc_conv_w, sc_norm_g, row_w):
    B, L, D = x.shape
    tm = min(PROJ_TOKENS, L)
    assert L % tm == 0 and tm % row_w == 0
    wz, wxbc, wdt, wgb, wgc, wv = w
    d_ssd, xbc_dim, nh2, d_conv = wz.shape[1], wxbc.shape[1], wdt.shape[1], wgb.shape[1]
    nh = nh2 // 2
    tok = lambda cols: pl.BlockSpec((1, tm, cols), lambda b, i: (b, i, 0))
    per_b = pl.BlockSpec((1, 1, D), lambda b, i: (b, 0, 0))
    return pl.pallas_call(
        functools.partial(_in_proj_kernel, row_w=row_w),
        grid=(B, L // tm),
        in_specs=[tok(D), per_b, per_b, _const_spec((1, D)),
                  _const_spec(wz.shape), _const_spec(wxbc.shape), _const_spec(wdt.shape),
                  _const_spec((nh2, D)), _const_spec(wgb.shape), _const_spec(wgc.shape), _const_spec(wv.shape),
                  _const_spec(conv_w.shape), _const_spec((1, xbc_dim)), _const_spec((1, nh2)),
                  _const_spec((nh2, 1)), _const_spec(sc_conv_w.shape), _const_spec((1, d_conv))],
        out_specs=[tok(d_ssd), tok(xbc_dim),
                   pl.BlockSpec((2, 1, tm, nh), lambda b, i: (0, b, i, 0)),
                   pl.BlockSpec((2, 1, nh, tm), lambda b, i: (0, b, 0, i)),
                   tok(d_conv)],
        out_shape=[jax.ShapeDtypeStruct((B, L, d_ssd), F32),
                   jax.ShapeDtypeStruct((B, L, xbc_dim), F32),
                   jax.ShapeDtypeStruct((2, B, L, nh), F32),
                   jax.ShapeDtypeStruct((2, B, nh, L), F32),
                   jax.ShapeDtypeStruct((B, L, d_conv), F32)],
        compiler_params=pltpu.CompilerParams(dimension_semantics=("parallel", "parallel"),
                                             vmem_limit_bytes=VMEM_LIMIT_BYTES),
        name="in_proj",
    )(x, sh, sc, g_pre.reshape(1, D), wz, wxbc, wdt, wdt.T, wgb, wgc, wv,
      conv_w, conv_b.reshape(1, xbc_dim), dt_bias.reshape(1, nh2), dt_bias.reshape(nh2, 1),
      sc_conv_w, sc_norm_g.reshape(1, d_conv))


def _ssd_kernel(xs_ref, bm_ref, cm_ref, dt_ref, dtT_ref, alog_ref, alogT_ref, h0_ref, y_ref, hf_ref, st_ref):
    d = pl.program_id(1)
    c = pl.program_id(2)
    nc = pl.num_programs(2)
    Q = SSD_CHUNK
    N = SSD_STATE
    P = SSD_HEAD_DIM
    d_ssd = xs_ref.shape[-1]
    nh = dt_ref.shape[-1]
    hpg = nh // SSD_GROUPS
    gw = hpg * P

    @pl.when(c == 0)
    def _():
        st_ref[...] = h0_ref[0, 0]

    a = dt_ref[0, 0] * (-jnp.exp(alog_ref[0]))
    aT = dtT_ref[0, 0] * (-jnp.exp(alogT_ref[0]))
    row = lax.broadcasted_iota(jnp.int32, (Q, Q), 0)
    col = lax.broadcasted_iota(jnp.int32, (Q, Q), 1)
    fwd = d == 0
    mask = (row - col) * (1 - 2 * d) >= 0
    maskb = mask.astype(F32).astype(MXU_DTYPE)
    acs = sum(_dot(maskb, p) for p in _split_bf16(a, 3))
    acsT = sum(_dot_nt(p, maskb) for p in _split_bf16(aT, 3))

    erow = lax.broadcasted_iota(jnp.int32, (nh, d_ssd), 0)
    ecol = lax.broadcasted_iota(jnp.int32, (nh, d_ssd), 1)
    expand = (ecol // P == erow).astype(F32).astype(MXU_DTYPE)
    acs_e = sum(_dot(p, expand) for p in _split_bf16(acs, 2))
    dt_e = sum(_dot(p, expand) for p in _split_bf16(dt_ref[0, 0], 2))
    tot_e = jnp.where(fwd, acs_e[Q - 1:Q, :], acs_e[0:1, :])

    xdt = xs_ref[0] * dt_e
    xw = (xdt * jnp.exp(tot_e - acs_e)).astype(MXU_DTYPE)
    xdtb = xdt.astype(MXU_DTYPE)
    eacs = jnp.exp(acs_e)
    cdec = jnp.exp(tot_e)
    lane = lax.broadcasted_iota(jnp.int32, (Q, 2 * P), 1)

    for g in range(SSD_GROUPS):
        bm = bm_ref[0, :, g * N:(g + 1) * N].astype(MXU_DTYPE)
        cm = cm_ref[0, :, g * N:(g + 1) * N].astype(MXU_DTYPE)
        cb = _dot_nt(cm, bm)
        hT = st_ref[g * N:(g + 1) * N, :]
        gcols = slice(g * gw, (g + 1) * gw)
        y_off = _dot(cm, hT.astype(MXU_DTYPE)) * eacs[:, gcols]
        st_ref[g * N:(g + 1) * N, :] = hT * cdec[:, gcols] + _dot_tn(bm, xw[:, gcols])
        for pr in range(hpg // 2):
            pcols = slice(g * gw + pr * 2 * P, g * gw + (pr + 1) * 2 * P)
            xpair = xdtb[:, pcols]
            halves = []
            for r in range(2):
                hd = g * hpg + pr * 2 + r
                seg = acs[:, hd:hd + 1] - acsT[hd:hd + 1, :]
                decay = jnp.exp(jnp.where(mask, seg, -jnp.inf))
                halves.append(_dot((cb * decay).astype(MXU_DTYPE), xpair))
            y_diag = jnp.where(lane < P, halves[0], halves[1])
            y_ref[0, 0, :, pcols] = y_diag + y_off[:, pr * 2 * P:(pr + 1) * 2 * P]

    @pl.when(c == nc - 1)
    def _():
        hf_ref[0, 0] = st_ref[...]


def _ssd_scan(xbc, dt, dtT, a_log, h0):
    B, L, _ = xbc.shape
    nh = dt.shape[-1]
    d_ssd = nh * SSD_HEAD_DIM
    gn = SSD_GROUPS * SSD_STATE
    Q = SSD_CHUNK
    nc = L // Q
    gw = d_ssd // SSD_GROUPS
    chunk = lambda d, c: c + d * (nc - 1 - 2 * c)
    return pl.pallas_call(
        _ssd_kernel,
        grid=(B, 2, nc),
        in_specs=[pl.BlockSpec((1, Q, d_ssd), lambda b, d, c: (b, chunk(d, c), 0)),
                  pl.BlockSpec((1, Q, gn), lambda b, d, c: (b, chunk(d, c), d_ssd // gn)),
                  pl.BlockSpec((1, Q, gn), lambda b, d, c: (b, chunk(d, c), d_ssd // gn + 1)),
                  pl.BlockSpec((1, 1, Q, nh), lambda b, d, c: (d, b, chunk(d, c), 0)),
                  pl.BlockSpec((1, 1, nh, Q), lambda b, d, c: (d, b, 0, chunk(d, c))),
                  pl.BlockSpec((1, 1, nh), lambda b, d, c: (d, 0, 0)),
                  pl.BlockSpec((1, nh, 1), lambda b, d, c: (d, 0, 0)),
                  pl.BlockSpec((1, 1, gn, gw), lambda b, d, c: (b, d, 0, 0))],
        out_specs=[pl.BlockSpec((1, 1, Q, d_ssd), lambda b, d, c: (d, b, chunk(d, c), 0)),
                   pl.BlockSpec((1, 1, gn, gw), lambda b, d, c: (b, d, 0, 0))],
        out_shape=[jax.ShapeDtypeStruct((2, B, L, d_ssd), F32),
                   jax.ShapeDtypeStruct((B, 2, gn, gw), F32)],
        scratch_shapes=[pltpu.VMEM((gn, gw), F32)],
        compiler_params=pltpu.CompilerParams(dimension_semantics=("parallel", "arbitrary", "arbitrary"),
                                             vmem_limit_bytes=VMEM_LIMIT_BYTES),
        name="ssd_scan",
    )(xbc, xbc, xbc, dt, dtT, a_log.reshape(2, 1, nh), a_log.reshape(2, nh, 1), h0)


def _mix_out_kernel(x_ref, yf_ref, yb_ref, xs_ref, z_ref, sco_ref, dsk_ref, ng_ref, wo1_ref, wo2_ref,
                    gpost_ref, gt_ref, gpre_ref, sh_ref, sc_ref, x1_ref, h2_ref):
    d_ssd = xs_ref.shape[-1]
    gw = d_ssd // SSD_GROUPS
    y = yf_ref[0, 0] + yb_ref[0, 0] + dsk_ref[...] * xs_ref[0]
    y = y * _silu(z_ref[0])
    parts = [_rms(y[:, g * gw:(g + 1) * gw]) for g in range(SSD_GROUPS)]
    yn = jnp.concatenate(parts, axis=-1) * ng_ref[...]
    mix = _dot(yn.astype(MXU_DTYPE), wo1_ref[...]) + _dot(sco_ref[0].astype(MXU_DTYPE), wo2_ref[...])
    x1 = x_ref[0] + gt_ref[0] * (_rms(mix) * gpost_ref[...])
    x1_ref[0] = x1
    h2 = _rms(x1) * gpre_ref[...]
    h2_ref[0] = h2 * (1.0 + sc_ref[0]) + sh_ref[0]


def _mix_out(x, y2, xbc, z, sco, d_skip_e, norm_g, wo1, wo2, g_post, gt1, g_pre_ffn, sh2, sc2):
    B, L, D = x.shape
    d_ssd = z.shape[-1]
    d_conv = sco.shape[-1]
    tm = min(PROJ_TOKENS, L)
    tok = lambda cols: pl.BlockSpec((1, tm, cols), lambda b, i: (b, i, 0))
    per_b = pl.BlockSpec((1, 1, D), lambda b, i: (b, 0, 0))
    ydir = lambda d: pl.BlockSpec((1, 1, tm, d_ssd), lambda b, i: (d, b, i, 0))
    return pl.pallas_call(
        _mix_out_kernel,
        grid=(B, L // tm),
        in_specs=[tok(D), ydir(0), ydir(1), tok(d_ssd), tok(d_ssd), tok(d_conv),
                  _const_spec((1, d_ssd)), _const_spec((1, d_ssd)), _const_spec(wo1.shape), _const_spec(wo2.shape),
                  _const_spec((1, D)), per_b, _const_spec((1, D)), per_b, per_b],
        out_specs=[tok(D), tok(D)],
        out_shape=[jax.ShapeDtypeStruct((B, L, D), F32), jax.ShapeDtypeStruct((B, L, D), F32)],
        compiler_params=pltpu.CompilerParams(dimension_semantics=("parallel", "parallel"),
                                             vmem_limit_bytes=VMEM_LIMIT_BYTES),
        name="mix_out",
    )(x, y2, y2, xbc, z, sco, d_skip_e.reshape(1, d_ssd), norm_g.reshape(1, d_ssd), wo1, wo2,
      g_post.reshape(1, D), gt1, g_pre_ffn.reshape(1, D), sh2, sc2)


def _topk_rows(s, k):
    n = s.shape[0]
    rows = lax.broadcasted_iota(jnp.int32, s.shape, 0).astype(F32)
    vals, idxs = [], []
    for _ in range(k):
        m = jnp.max(s, axis=0, keepdims=True)
        i = jnp.min(jnp.where(s == m, rows, float(n)), axis=0, keepdims=True)
        vals.append(m)
        idxs.append(i)
        s = jnp.where(rows == i, -jnp.inf, s)
    return jnp.concatenate(vals, axis=0), jnp.concatenate(idxs, axis=0)


def _pair_candidates(k):
    return [(i, j) for i in range(k) for j in range(k) if (i + 1) * (j + 1) <= k]


def _pair_select_matrices(k):
    pairs = _pair_candidates(k)
    rows = -(-len(pairs) // 8) * 8
    sel0 = np.zeros((rows, k), np.float32)
    sel1 = np.zeros((rows, k), np.float32)
    bias = np.full((rows, 1), -np.inf, np.float32)
    for r, (i, j) in enumerate(pairs):
        sel0[r, i] = 1.0
        sel1[r, j] = 1.0
        bias[r, 0] = 0.0
    return sel0, sel1, bias


def _select_rows(sel, x):
    p = _split_bf16(x, 3)
    return (_dot(sel, p[0]) + _dot(sel, p[1])) + _dot(sel, p[2])


def _route_kernel(h2_ref, wq_ref, keys_ref, sel0_ref, sel1_ref, bias_ref, eid_ref, g_ref):
    K = PEER_TOPK
    NK = PEER_N_KEYS
    hb = h2_ref[...].astype(MXU_DTYPE)
    T = hb.shape[0]
    half = keys_ref.shape[-1]
    sel0 = sel0_ref[...]
    sel1 = sel1_ref[...]
    n_cand = sel0.shape[0]
    c_rows = lax.broadcasted_iota(jnp.int32, (n_cand, T), 0).astype(F32)

    def head(h, carry):
        q = _dot(hb, wq_ref[h]).astype(MXU_DTYPE)
        sv, si = [], []
        for p in range(2):
            s = _dot_nt(keys_ref[h, p], q[:, p * half:(p + 1) * half])
            v, i = _topk_rows(s, K)
            sv.append(v)
            si.append(i)
        cand = _select_rows(sel0, sv[0]) + _select_rows(sel1, sv[1]) + bias_ref[...]
        ecand = _dot(sel0, si[0].astype(MXU_DTYPE)) * float(NK) + _dot(sel1, si[1].astype(MXU_DTYPE))
        cv, eid = [], []
        for _ in range(K):
            m = jnp.max(cand, axis=0, keepdims=True)
            i = jnp.min(jnp.where(cand == m, c_rows, float(n_cand)), axis=0, keepdims=True)
            hit = c_rows == i
            cv.append(m)
            eid.append(jnp.max(jnp.where(hit, ecand, -1.0), axis=0, keepdims=True))
            cand = jnp.where(hit, -jnp.inf, cand)
        cvs = jnp.concatenate(cv, axis=0)
        ex = jnp.exp(cvs - cv[0])
        gate = ex / jnp.sum(ex, axis=0, keepdims=True)
        rows = pl.ds(pl.multiple_of(h * K, K), K)
        eid_ref[rows, :] = jnp.concatenate(eid, axis=0).astype(jnp.int32)
        g_ref[rows, :] = gate
        return carry

    lax.fori_loop(0, PEER_HEADS, head, 0)


def _peer_route(h2, wq, keys):
    ntok, D = h2.shape
    T = ROUTE_TOKENS
    slots = PEER_HEADS * PEER_TOPK
    sel0, sel1, bias = _pair_select_matrices(PEER_TOPK)
    sel0 = jnp.asarray(sel0, MXU_DTYPE)
    sel1 = jnp.asarray(sel1, MXU_DTYPE)
    return pl.pallas_call(
        _route_kernel,
        grid=(ntok // T,),
        in_specs=[pl.BlockSpec((T, D), lambda i: (i, 0)), _const_spec(wq.shape), _const_spec(keys.shape),
                  _const_spec(sel0.shape), _const_spec(sel1.shape), _const_spec(bias.shape)],
        out_specs=[pl.BlockSpec((slots, T), lambda i: (0, i)), pl.BlockSpec((slots, T), lambda i: (0, i))],
        out_shape=[jax.ShapeDtypeStruct((slots, ntok), jnp.int32), jax.ShapeDtypeStruct((slots, ntok), F32)],
        compiler_params=pltpu.CompilerParams(dimension_semantics=("parallel",),
                                             vmem_limit_bytes=VMEM_LIMIT_BYTES),
        name="peer_route",
    )(h2, wq, keys, sel0, sel1, jnp.asarray(bias))


def _unpack_bf16_pair(w):
    lo = lax.bitcast_convert_type(w << 16, F32)
    hi = lax.bitcast_convert_type(w & jnp.uint32(0xFFFF0000), F32)
    return lo, hi


def _peer_kernel(eid_hbm, tab_hbm, x_ref, g_ref, o_ref, *scratch):
    R = PEER_RING
    e_bufs, bufs, rsem, esem = scratch[:2], scratch[2:2 + R], scratch[2 + R], scratch[3 + R]
    i = pl.program_id(0)
    n = pl.num_programs(0)
    T, D = x_ref.shape
    S = g_ref.shape[-1]
    C = D // (2 * LANES)
    G = T // R
    gsz = R * S

    def eid_copy(grp, par):
        return pltpu.make_async_copy(eid_hbm.at[pl.ds(grp * gsz, gsz)], e_bufs[par], esem.at[par])

    def issue(slot, par, tok):
        for k in range(S):
            e = e_bufs[par][tok * S + k]
            pltpu.async_copy(tab_hbm.at[pl.ds(e * (2 * C), 2 * C), :], bufs[slot].at[:, k, :], rsem.at[slot],
                             priority=k % 2)

    def wait(slot):
        pltpu.make_async_copy(bufs[slot], bufs[slot], rsem.at[slot]).wait()

    @pl.when(i == 0)
    def _():
        eid_copy(0, 0).start()
        eid_copy(1, 1).start()
        eid_copy(0, 0).wait()
        for t in range(R - 1):
            issue(t, 0, t)

    eye = lax.broadcasted_iota(jnp.int32, (S, S), 0) == lax.broadcasted_iota(jnp.int32, (S, S), 1)

    def evaluate(slot, t):
        buf = bufs[slot]
        x_row = x_ref[pl.ds(t, 1), :]
        g_col = jnp.sum(jnp.where(eye, g_ref[pl.ds(t, 1), :], 0.0), axis=-1, keepdims=True)
        acc = None
        for c in range(C):
            lo, hi = _unpack_bf16_pair(buf[c])
            part = lo * x_row[:, c * LANES:(c + 1) * LANES] + hi * x_row[:, D // 2 + c * LANES:D // 2 + (c + 1) * LANES]
            acc = part if acc is None else acc + part
        w = g_col * _gelu_tanh(jnp.sum(acc, axis=-1, keepdims=True))
        outs_lo, outs_hi = [], []
        for c in range(C):
            lo, hi = _unpack_bf16_pair(buf[C + c])
            outs_lo.append(jnp.sum(w * lo, axis=0, keepdims=True))
            outs_hi.append(jnp.sum(w * hi, axis=0, keepdims=True))
        o_ref[pl.ds(t, 1), :] = jnp.concatenate(outs_lo + outs_hi, axis=-1)

    def group(grp, par, t0):
        for j in range(R):
            wait(j)
            if j == 0:
                issue(R - 1, par, R - 1)
                eid_copy(grp + 2, par).start()
            else:
                if j == 1:
                    eid_copy(grp + 1, 1 - par).wait()
                issue(j - 1, 1 - par, j - 1)
            evaluate(j, t0 + j)

    def body(gp, carry):
        for par in range(2):
            gi = 2 * gp + par
            group(i * G + gi, par, pl.multiple_of(gi * R, R))
        return carry

    lax.fori_loop(0, G // 2, body, 0)

    @pl.when(i == n - 1)
    def _():
        for j in range(R - 1):
            wait(j)
        eid_copy(n * G + 1, 1).wait()


def _peer_experts(eid_flat, g, h2, tab):
    ntok, D = h2.shape
    S = g.shape[-1]
    T = PEER_TOK_BLOCK
    C = D // (2 * LANES)
    assert T % (2 * PEER_RING) == 0
    return pl.pallas_call(
        _peer_kernel,
        grid=(ntok // T,),
        in_specs=[pl.BlockSpec(memory_space=pl.ANY),
                  pl.BlockSpec(memory_space=pl.ANY),
                  pl.BlockSpec((T, D), lambda i: (i, 0)),
                  pl.BlockSpec((T, S), lambda i: (i, 0))],
        out_specs=pl.BlockSpec((T, D), lambda i: (i, 0)),
        out_shape=jax.ShapeDtypeStruct((ntok, D), F32),
        scratch_shapes=[pltpu.SMEM((PEER_RING * S,), jnp.int32) for _ in range(2)]
        + [pltpu.VMEM((2 * C, S, LANES), jnp.uint32) for _ in range(PEER_RING)]
        + [pltpu.SemaphoreType.DMA((PEER_RING,)), pltpu.SemaphoreType.DMA((2,))],
        compiler_params=pltpu.CompilerParams(dimension_semantics=("arbitrary",),
                                             vmem_limit_bytes=VMEM_LIMIT_BYTES),
        name="peer_experts",
    )(eid_flat, tab, h2, g)


def _final_kernel(x1_ref, p_ref, gt_ref, gpost_ref, o_ref):
    o_ref[0] = x1_ref[0] + gt_ref[0] * (_rms(p_ref[0]) * gpost_ref[...])


def _final(x1, peer_out, gt2, g_post):
    B, L, D = x1.shape
    tm = min(2 * PROJ_TOKENS, L)
    tok = pl.BlockSpec((1, tm, D), lambda b, i: (b, i, 0))
    return pl.pallas_call(
        _final_kernel,
        grid=(B, L // tm),
        in_specs=[tok, tok, pl.BlockSpec((1, 1, D), lambda b, i: (b, 0, 0)), _const_spec((1, D))],
        out_specs=tok,
        out_shape=jax.ShapeDtypeStruct((B, L, D), F32),
        compiler_params=pltpu.CompilerParams(dimension_semantics=("parallel", "parallel")),
        name="final",
    )(x1, peer_out, gt2, g_post.reshape(1, D))


def _pack_bf16_pairs(a):
    bits = lax.bitcast_convert_type(a.astype(jnp.bfloat16), jnp.uint16).astype(jnp.uint32)
    half = a.shape[1] // 2
    return bits[:, :half] | (bits[:, half:] << 16)


def _split_w_in(w_in, d_ssd, xbc_dim, nh2, d_conv):
    bounds = [0, d_ssd, d_ssd + xbc_dim, d_ssd + xbc_dim + nh2]
    bounds += [bounds[-1] + d_conv, bounds[-1] + 2 * d_conv, bounds[-1] + 3 * d_conv]
    wz, wxbc, wdt, wgb, wgc, wv = (w_in[:, bounds[i]:bounds[i + 1]].astype(MXU_DTYPE) for i in range(6))
    return wz, wxbc, wdt, wgb, wgc, wv


def kernel(x, c, ctx, c_ctx, w_ada, b_ada, g_pre_mix, g_post_mix, g_pre_ffn, g_post_ffn, w_in, ssd_conv_w, ssd_conv_b, ssd_dt_bias, ssd_a_log, ssd_d, ssd_norm_g, sc_conv_w, sc_norm_g, w_out, peer_w_q, peer_keys, peer_u, peer_v):
    B, L, D = x.shape
    ctx_len = ctx.shape[1]
    assert w_ada.shape[0] == 1, "single-layer block"
    nh2 = ssd_dt_bias.shape[1] * ssd_dt_bias.shape[2]
    nh = nh2 // 2
    d_ssd = nh * SSD_HEAD_DIM
    xbc_dim = ssd_conv_w.shape[-1]
    d_conv = sc_conv_w.shape[-1]
    n_mod = w_ada.shape[-1] // D

    pad_rows = (-(B + 1)) % 8
    cc = jnp.concatenate([c, c_ctx[None, :], jnp.zeros((pad_rows, D), F32)], axis=0)
    mod = _adaln(cc, w_ada[0], b_ada[0])
    mod_l = mod[:B].reshape(B, n_mod, 1, D)
    mod_c = jnp.broadcast_to(mod[B].reshape(1, n_mod, 1, D), (B, n_mod, 1, D))
    sh1_l, sc1_l, gt1_l, sh2_l, sc2_l, gt2_l = (mod_l[:, j] for j in range(n_mod))
    sh1_c, sc1_c = mod_c[:, 0], mod_c[:, 1]

    w = _split_w_in(w_in[0], d_ssd, xbc_dim, nh2, d_conv)
    proj_args = (w, ssd_conv_w[0], ssd_conv_b[0], ssd_dt_bias[0].reshape(nh2), sc_conv_w[0], sc_norm_g[0])
    a_log = ssd_a_log[0]

    _, xbc_c, dt_c, dtT_c, _ = _in_proj(ctx, sh1_c, sc1_c, g_pre_mix[0], *proj_args, row_w=ctx_len)
    gn = SSD_GROUPS * SSD_STATE
    h0 = jnp.zeros((B, 2, gn, d_ssd // SSD_GROUPS), F32)
    _, h_ctx = _ssd_scan(xbc_c, dt_c, dtT_c, a_log, h0)

    z_l, xbc_l, dt_l, dtT_l, sco_l = _in_proj(x, sh1_l, sc1_l, g_pre_mix[0], *proj_args, row_w=GRID_W)
    y2, _ = _ssd_scan(xbc_l, dt_l, dtT_l, a_log, h_ctx)

    wo = w_out[0].astype(MXU_DTYPE)
    d_skip_e = jnp.repeat(ssd_d[0], SSD_HEAD_DIM)
    x1, h2 = _mix_out(x, y2, xbc_l, z_l, sco_l, d_skip_e, ssd_norm_g[0], wo[:d_ssd], wo[d_ssd:],
                      g_post_mix[0], gt1_l, g_pre_ffn[0], sh2_l, sc2_l)

    ntok = B * L
    qd = peer_w_q.shape[-1] // PEER_HEADS
    wq = peer_w_q[0].reshape(D, PEER_HEADS, qd).transpose(1, 0, 2).astype(MXU_DTYPE)
    eidT, gT = _peer_route(h2.reshape(ntok, D), wq, peer_keys[0].astype(MXU_DTYPE))
    tab = jnp.concatenate([_pack_bf16_pairs(peer_u[0]), _pack_bf16_pairs(peer_v[0])], axis=1).reshape(-1, LANES)
    eid = jnp.pad(eidT.T.reshape(-1), (0, 2 * PEER_RING * eidT.shape[0]))
    peer_out = _peer_experts(eid, gT.T, h2.reshape(ntok, D), tab)

    return _final(x1, peer_out.reshape(B, L, D), gt2_l, g_post_ffn[0])
```

```python
import functools

import jax
import jax.numpy as jnp
import numpy as np
from jax import lax
from jax.experimental import pallas as pl
from jax.experimental.pallas import tpu as pltpu
from jax.experimental.pallas import tpu_sc as plsc

GRID_W = 64
SSD_HEAD_DIM = 64
SSD_GROUPS = 4
SSD_STATE = 128
SSD_CHUNK = 128
PEER_HEADS = 8
PEER_N_KEYS = 128
PEER_TOPK = 16
EPS = 1e-6
LANES = 128

MXU_DTYPE = jnp.bfloat16
VMEM_LIMIT_BYTES = 56 * 1024 * 1024

PROJ_TOKENS = 256
ROUTE_TOKENS = 1024
PEER_TOK_BLOCK = 128
PEER_RING = 8
SC_CORES, SC_SUBCORES = 2, 16
SC_ROWS = 64
SC_TOKEN_SHARE = 10240
STAGED_TOK_BLOCK = 8

F32 = jnp.float32


def _dot(a, b):
    return jnp.dot(a, b, preferred_element_type=F32)


def _dot_nt(a, b):
    return lax.dot_general(a, b, (((1,), (1,)), ((), ())), preferred_element_type=F32)


def _dot_tn(a, b):
    return lax.dot_general(a, b, (((0,), (0,)), ((), ())), preferred_element_type=F32)


def _split_bf16(a, terms):
    parts = []
    r = a
    for i in range(terms):
        p = r.astype(MXU_DTYPE)
        parts.append(p)
        if i + 1 < terms:
            r = r - p.astype(F32)
    return parts


def _silu(x):
    return x * jax.nn.sigmoid(x)


def _softplus(x):
    return jnp.maximum(x, 0.0) + jnp.log1p(jnp.exp(-jnp.abs(x)))


def _gelu_tanh(x):
    return 0.5 * x * (1.0 + jnp.tanh(0.7978845608028654 * (x + 0.044715 * (x * x * x))))


def _rms(x):
    return x * lax.rsqrt(jnp.mean(x * x, axis=-1, keepdims=True) + EPS)


def _const_spec(shape):
    zeros = (0,) * len(shape)
    return pl.BlockSpec(shape, lambda *_: zeros, pipeline_mode=pl.Buffered(1))


def _adaln_kernel(c_ref, w_ref, b_ref, o_ref):
    s = _silu(c_ref[...]).astype(MXU_DTYPE)
    o_ref[...] = _dot(s, w_ref[...].astype(MXU_DTYPE)) + b_ref[...]


def _adaln(cc, w, b):
    rows, d = cc.shape
    n = w.shape[1]
    bn = 1024
    return pl.pallas_call(
        _adaln_kernel,
        grid=(n // bn,),
        in_specs=[pl.BlockSpec((rows, d), lambda j: (0, 0)),
                  pl.BlockSpec((d, bn), lambda j: (0, j)),
                  pl.BlockSpec((1, bn), lambda j: (0, j))],
        out_specs=pl.BlockSpec((rows, bn), lambda j: (0, j)),
        out_shape=jax.ShapeDtypeStruct((rows, n), F32),
        name="adaln",
    )(cc, w, b.reshape(1, n))


def _dwconv(u, w, pos, row_w):
    taps = w.shape[0]
    p = taps // 2
    n = u.shape[0]
    out = u * w[p:p + 1, :]
    for k in range(taps):
        d = k - p
        if d == 0:
            continue
        shifted = pltpu.roll(u, (-d) % n, axis=0)
        valid = jnp.logical_and(pos + d >= 0, pos + d < row_w)
        out = out + jnp.where(valid, shifted, 0.0) * w[k:k + 1, :]
    return out


def _in_proj_kernel(x_ref, sh_ref, sc_ref, gpre_ref, wz_ref, wxbc_ref, wdt_ref, wdtT_ref, wgb_ref, wgc_ref,
                    wv_ref, cw_ref, cb_ref, dtb_ref, dtbT_ref, scw_ref, scg_ref,
                    z_ref, xbc_ref, dt_ref, dtT_ref, sco_ref, *, row_w):
    x = x_ref[0]
    tm = x.shape[0]
    h = _rms(x) * gpre_ref[...]
    h = h * (1.0 + sc_ref[0]) + sh_ref[0]
    hb = h.astype(MXU_DTYPE)
    pos = lax.broadcasted_iota(jnp.int32, (tm, 1), 0) % row_w

    z_ref[0] = _dot(hb, wz_ref[...])
    xbc = _dwconv(_dot(hb, wxbc_ref[...]), cw_ref[...], pos, row_w) + cb_ref[...]
    xbc_ref[0] = _silu(xbc)

    nh = dt_ref.shape[-1]
    dt = _softplus(_dot(hb, wdt_ref[...]) + dtb_ref[...])
    dt_ref[0, 0] = dt[:, :nh]
    dt_ref[1, 0] = dt[:, nh:]
    dtT = _softplus(_dot_nt(wdtT_ref[...], hb) + dtbT_ref[...])
    dtT_ref[0, 0] = dtT[:nh]
    dtT_ref[1, 0] = dtT[nh:]

    gcv = _dot(hb, wgc_ref[...]) * _dot(hb, wv_ref[...])
    sc = _dot(hb, wgb_ref[...]) * _dwconv(gcv, scw_ref[...], pos, row_w)
    sco_ref[0] = _rms(sc) * scg_ref[...]


def _in_proj(x, sh, sc, g_pre, w, conv_w, conv_b, dt_bias, sc_conv_w, sc_norm_g, row_w):
    B, L, D = x.shape
    tm = min(PROJ_TOKENS, L)
    assert L % tm == 0 and tm % row_w == 0
    wz, wxbc, wdt, wgb, wgc, wv = w
    d_ssd, xbc_dim, nh2, d_conv = wz.shape[1], wxbc.shape[1], wdt.shape[1], wgb.shape[1]
    nh = nh2 // 2
    tok = lambda cols: pl.BlockSpec((1, tm, cols), lambda b, i: (b, i, 0))
    per_b = pl.BlockSpec((1, 1, D), lambda b, i: (b, 0, 0))
    return pl.pallas_call(
        functools.partial(_in_proj_kernel, row_w=row_w),
        grid=(B, L // tm),
        in_specs=[tok(D), per_b, per_b, _const_spec((1, D)),
                  _const_spec(wz.shape), _const_spec(wxbc.shape), _const_spec(wdt.shape),
                  _const_spec((nh2, D)), _const_spec(wgb.shape), _const_spec(wgc.shape), _const_spec(wv.shape),
                  _const_spec(conv_w.shape), _const_spec((1, xbc_dim)), _const_spec((1, nh2)),
                  _const_spec((nh2, 1)), _const_spec(sc_conv_w.shape), _const_spec((1, d_conv))],
        out_specs=[tok(d_ssd), tok(xbc_dim),
                   pl.BlockSpec((2, 1, tm, nh), lambda b, i: (0, b, i, 0)),
                   pl.BlockSpec((2, 1, nh, tm), lambda b, i: (0, b, 0, i)),
                   tok(d_conv)],
        out_shape=[jax.ShapeDtypeStruct((B, L, d_ssd), F32),
                   jax.ShapeDtypeStruct((B, L, xbc_dim), F32),
                   jax.ShapeDtypeStruct((2, B, L, nh), F32),
                   jax.ShapeDtypeStruct((2, B, nh, L), F32),
                   jax.ShapeDtypeStruct((B, L, d_conv), F32)],
        compiler_params=pltpu.CompilerParams(dimension_semantics=("parallel", "parallel"),
                                             vmem_limit_bytes=VMEM_LIMIT_BYTES),
        name="in_proj",
    )(x, sh, sc, g_pre.reshape(1, D), wz, wxbc, wdt, wdt.T, wgb, wgc, wv,
      conv_w, conv_b.reshape(1, xbc_dim), dt_bias.reshape(1, nh2), dt_bias.reshape(nh2, 1),
      sc_conv_w, sc_norm_g.reshape(1, d_conv))


def _ssd_kernel(xs_ref, bm_ref, cm_ref, dt_ref, dtT_ref, alog_ref, alogT_ref, h0_ref, y_ref, hf_ref, st_ref):
    d = pl.program_id(1)
    c = pl.program_id(2)
    nc = pl.num_programs(2)
    Q = SSD_CHUNK
    N = SSD_STATE
    P = SSD_HEAD_DIM
    d_ssd = xs_ref.shape[-1]
    nh = dt_ref.shape[-1]
    hpg = nh // SSD_GROUPS
    gw = hpg * P

    @pl.when(c == 0)
    def _():
        st_ref[...] = h0_ref[0, 0]

    a = dt_ref[0, 0] * (-jnp.exp(alog_ref[0]))
    aT = dtT_ref[0, 0] * (-jnp.exp(alogT_ref[0]))
    row = lax.broadcasted_iota(jnp.int32, (Q, Q), 0)
    col = lax.broadcasted_iota(jnp.int32, (Q, Q), 1)
    fwd = d == 0
    mask = (row - col) * (1 - 2 * d) >= 0
    maskb = mask.astype(F32).astype(MXU_DTYPE)
    acs = sum(_dot(maskb, p) for p in _split_bf16(a, 3))
    acsT = sum(_dot_nt(p, maskb) for p in _split_bf16(aT, 3))

    erow = lax.broadcasted_iota(jnp.int32, (nh, d_ssd), 0)
    ecol = lax.broadcasted_iota(jnp.int32, (nh, d_ssd), 1)
    expand = (ecol // P == erow).astype(F32).astype(MXU_DTYPE)
    acs_e = sum(_dot(p, expand) for p in _split_bf16(acs, 2))
    dt_e = sum(_dot(p, expand) for p in _split_bf16(dt_ref[0, 0], 2))
    tot_e = jnp.where(fwd, acs_e[Q - 1:Q, :], acs_e[0:1, :])

    xdt = xs_ref[0] * dt_e
    xw = (xdt * jnp.exp(tot_e - acs_e)).astype(MXU_DTYPE)
    xdtb = xdt.astype(MXU_DTYPE)
    eacs = jnp.exp(acs_e)
    cdec = jnp.exp(tot_e)
    lane = lax.broadcasted_iota(jnp.int32, (Q, 2 * P), 1)

    for g in range(SSD_GROUPS):
        bm = bm_ref[0, :, g * N:(g + 1) * N].astype(MXU_DTYPE)
        cm = cm_ref[0, :, g * N:(g + 1) * N].astype(MXU_DTYPE)
        cb = _dot_nt(cm, bm)
        hT = st_ref[g * N:(g + 1) * N, :]
        gcols = slice(g * gw, (g + 1) * gw)
        y_off = _dot(cm, hT.astype(MXU_DTYPE)) * eacs[:, gcols]
        st_ref[g * N:(g + 1) * N, :] = hT * cdec[:, gcols] + _dot_tn(bm, xw[:, gcols])
        for pr in range(hpg // 2):
            pcols = slice(g * gw + pr * 2 * P, g * gw + (pr + 1) * 2 * P)
            xpair = xdtb[:, pcols]
            halves = []
            for r in range(2):
                hd = g * hpg + pr * 2 + r
                seg = acs[:, hd:hd + 1] - acsT[hd:hd + 1, :]
                decay = jnp.exp(jnp.where(mask, seg, -jnp.inf))
                halves.append(_dot((cb * decay).astype(MXU_DTYPE), xpair))
            y_diag = jnp.where(lane < P, halves[0], halves[1])
            y_ref[0, 0, :, pcols] = y_diag + y_off[:, pr * 2 * P:(pr + 1) * 2 * P]

    @pl.when(c == nc - 1)
    def _():
        hf_ref[0, 0] = st_ref[...]


def _ssd_scan(xbc, dt, dtT, a_log, h0):
    B, L, _ = xbc.shape
    nh = dt.shape[-1]
    d_ssd = nh * SSD_HEAD_DIM
    gn = SSD_GROUPS * SSD_STATE
    Q = SSD_CHUNK
    nc = L // Q
    gw = d_ssd // SSD_GROUPS
    chunk = lambda d, c: c + d * (nc - 1 - 2 * c)
    return pl.pallas_call(
        _ssd_kernel,
        grid=(B, 2, nc),
        in_specs=[pl.BlockSpec((1, Q, d_ssd), lambda b, d, c: (b, chunk(d, c), 0)),
                  pl.BlockSpec((1, Q, gn), lambda b, d, c: (b, chunk(d, c), d_ssd // gn)),
                  pl.BlockSpec((1, Q, gn), lambda b, d, c: (b, chunk(d, c), d_ssd // gn + 1)),
                  pl.BlockSpec((1, 1, Q, nh), lambda b, d, c: (d, b, chunk(d, c), 0)),
                  pl.BlockSpec((1, 1, nh, Q), lambda b, d, c: (d, b, 0, chunk(d, c))),
                  pl.BlockSpec((1, 1, nh), lambda b, d, c: (d, 0, 0)),
                  pl.BlockSpec((1, nh, 1), lambda b, d, c: (d, 0, 0)),
                  pl.BlockSpec((1, 1, gn, gw), lambda b, d, c: (b, d, 0, 0))],
        out_specs=[pl.BlockSpec((1, 1, Q, d_ssd), lambda b, d, c: (d, b, chunk(d, c), 0)),
                   pl.BlockSpec((1, 1, gn, gw), lambda b, d, c: (b, d, 0, 0))],
        out_shape=[jax.ShapeDtypeStruct((2, B, L, d_ssd), F32),
                   jax.ShapeDtypeStruct((B, 2, gn, gw), F32)],
        scratch_shapes=[pltpu.VMEM((gn, gw), F32)],
        compiler_params=pltpu.CompilerParams(dimension_semantics=("parallel", "arbitrary", "arbitrary"),
                                             vmem_limit_bytes=VMEM_LIMIT_BYTES),
        name="ssd_scan",
    )(xbc, xbc, xbc, dt, dtT, a_log.reshape(2, 1, nh), a_log.reshape(2, nh, 1), h0)


def _mix_out_kernel(x_ref, yf_ref, yb_ref, xs_ref, z_ref, sco_ref, dsk_ref, ng_ref, wo1_ref, wo2_ref,
                    gpost_ref, gt_ref, gpre_ref, sh_ref, sc_ref, x1_ref, h2_ref):
    d_ssd = xs_ref.shape[-1]
    gw = d_ssd // SSD_GROUPS
    y = yf_ref[0, 0] + yb_ref[0, 0] + dsk_ref[...] * xs_ref[0]
    y = y * _silu(z_ref[0])
    parts = [_rms(y[:, g * gw:(g + 1) * gw]) for g in range(SSD_GROUPS)]
    yn = jnp.concatenate(parts, axis=-1) * ng_ref[...]
    mix = _dot(yn.astype(MXU_DTYPE), wo1_ref[...]) + _dot(sco_ref[0].astype(MXU_DTYPE), wo2_ref[...])
    x1 = x_ref[0] + gt_ref[0] * (_rms(mix) * gpost_ref[...])
    x1_ref[0] = x1
    h2 = _rms(x1) * gpre_ref[...]
    h2_ref[0] = h2 * (1.0 + sc_ref[0]) + sh_ref[0]


def _mix_out(x, y2, xbc, z, sco, d_skip_e, norm_g, wo1, wo2, g_post, gt1, g_pre_ffn, sh2, sc2):
    B, L, D = x.shape
    d_ssd = z.shape[-1]
    d_conv = sco.shape[-1]
    tm = min(PROJ_TOKENS, L)
    tok = lambda cols: pl.BlockSpec((1, tm, cols), lambda b, i: (b, i, 0))
    per_b = pl.BlockSpec((1, 1, D), lambda b, i: (b, 0, 0))
    ydir = lambda d: pl.BlockSpec((1, 1, tm, d_ssd), lambda b, i: (d, b, i, 0))
    return pl.pallas_call(
        _mix_out_kernel,
        grid=(B, L // tm),
        in_specs=[tok(D), ydir(0), ydir(1), tok(d_ssd), tok(d_ssd), tok(d_conv),
                  _const_spec((1, d_ssd)), _const_spec((1, d_ssd)), _const_spec(wo1.shape), _const_spec(wo2.shape),
                  _const_spec((1, D)), per_b, _const_spec((1, D)), per_b, per_b],
        out_specs=[tok(D), tok(D)],
        out_shape=[jax.ShapeDtypeStruct((B, L, D), F32), jax.ShapeDtypeStruct((B, L, D), F32)],
        compiler_params=pltpu.CompilerParams(dimension_semantics=("parallel", "parallel"),
                                             vmem_limit_bytes=VMEM_LIMIT_BYTES),
        name="mix_out",
    )(x, y2, y2, xbc, z, sco, d_skip_e.reshape(1, d_ssd), norm_g.reshape(1, d_ssd), wo1, wo2,
      g_post.reshape(1, D), gt1, g_pre_ffn.reshape(1, D), sh2, sc2)


def _topk_rows(s, k):
    n = s.shape[0]
    rows = lax.broadcasted_iota(jnp.int32, s.shape, 0).astype(F32)
    vals, idxs = [], []
    for _ in range(k):
        m = jnp.max(s, axis=0, keepdims=True)
        i = jnp.min(jnp.where(s == m, rows, float(n)), axis=0, keepdims=True)
        vals.append(m)
        idxs.append(i)
        s = jnp.where(rows == i, -jnp.inf, s)
    return jnp.concatenate(vals, axis=0), jnp.concatenate(idxs, axis=0)


def _pair_candidates(k):
    return [(i, j) for i in range(k) for j in range(k) if (i + 1) * (j + 1) <= k]


def _pair_select_matrices(k):
    pairs = _pair_candidates(k)
    rows = -(-len(pairs) // 8) * 8
    sel0 = np.zeros((rows, k), np.float32)
    sel1 = np.zeros((rows, k), np.float32)
    bias = np.full((rows, 1), -np.inf, np.float32)
    for r, (i, j) in enumerate(pairs):
        sel0[r, i] = 1.0
        sel1[r, j] = 1.0
        bias[r, 0] = 0.0
    return sel0, sel1, bias


def _select_rows(sel, x):
    p = _split_bf16(x, 3)
    return (_dot(sel, p[0]) + _dot(sel, p[1])) + _dot(sel, p[2])


def _route_kernel(h2_ref, wq_ref, keys_ref, sel0_ref, sel1_ref, bias_ref, eid_ref, g_ref):
    K = PEER_TOPK
    NK = PEER_N_KEYS
    hb = h2_ref[...].astype(MXU_DTYPE)
    T = hb.shape[0]
    half = keys_ref.shape[-1]
    sel0 = sel0_ref[...]
    sel1 = sel1_ref[...]
    n_cand = sel0.shape[0]
    c_rows = lax.broadcasted_iota(jnp.int32, (n_cand, T), 0).astype(F32)

    def head(h, carry):
        q = _dot(hb, wq_ref[h]).astype(MXU_DTYPE)
        sv, si = [], []
        for p in range(2):
            s = _dot_nt(keys_ref[h, p], q[:, p * half:(p + 1) * half])
            v, i = _topk_rows(s, K)
            sv.append(v)
            si.append(i)
        cand = _select_rows(sel0, sv[0]) + _select_rows(sel1, sv[1]) + bias_ref[...]
        ecand = _dot(sel0, si[0].astype(MXU_DTYPE)) * float(NK) + _dot(sel1, si[1].astype(MXU_DTYPE))
        cv, eid = [], []
        for _ in range(K):
            m = jnp.max(cand, axis=0, keepdims=True)
            i = jnp.min(jnp.where(cand == m, c_rows, float(n_cand)), axis=0, keepdims=True)
            hit = c_rows == i
            cv.append(m)
            eid.append(jnp.max(jnp.where(hit, ecand, -1.0), axis=0, keepdims=True))
            cand = jnp.where(hit, -jnp.inf, cand)
        cvs = jnp.concatenate(cv, axis=0)
        ex = jnp.exp(cvs - cv[0])
        gate = ex / jnp.sum(ex, axis=0, keepdims=True)
        rows = pl.ds(pl.multiple_of(h * K, K), K)
        eid_ref[rows, :] = jnp.concatenate(eid, axis=0).astype(jnp.int32)
        g_ref[rows, :] = gate
        return carry

    lax.fori_loop(0, PEER_HEADS, head, 0)


def _peer_route(h2, wq, keys):
    ntok, D = h2.shape
    T = ROUTE_TOKENS
    slots = PEER_HEADS * PEER_TOPK
    sel0, sel1, bias = _pair_select_matrices(PEER_TOPK)
    sel0 = jnp.asarray(sel0, MXU_DTYPE)
    sel1 = jnp.asarray(sel1, MXU_DTYPE)
    return pl.pallas_call(
        _route_kernel,
        grid=(ntok // T,),
        in_specs=[pl.BlockSpec((T, D), lambda i: (i, 0)), _const_spec(wq.shape), _const_spec(keys.shape),
                  _const_spec(sel0.shape), _const_spec(sel1.shape), _const_spec(bias.shape)],
        out_specs=[pl.BlockSpec((slots, T), lambda i: (0, i)), pl.BlockSpec((slots, T), lambda i: (0, i))],
        out_shape=[jax.ShapeDtypeStruct((slots, ntok), jnp.int32), jax.ShapeDtypeStruct((slots, ntok), F32)],
        compiler_params=pltpu.CompilerParams(dimension_semantics=("parallel",),
                                             vmem_limit_bytes=VMEM_LIMIT_BYTES),
        name="peer_route",
    )(h2, wq, keys, sel0, sel1, jnp.asarray(bias))


def _unpack_bf16_pair(w):
    lo = lax.bitcast_convert_type(w << 16, F32)
    hi = lax.bitcast_convert_type(w & jnp.uint32(0xFFFF0000), F32)
    return lo, hi


def _expert_eval(chunk, x_row, g_row, eye):
    D = x_row.shape[-1]
    C = D // (2 * LANES)
    g_col = jnp.sum(jnp.where(eye, g_row, 0.0), axis=-1, keepdims=True)
    acc = None
    for c in range(C):
        lo, hi = _unpack_bf16_pair(chunk(c))
        part = lo * x_row[:, c * LANES:(c + 1) * LANES] + hi * x_row[:, D // 2 + c * LANES:D // 2 + (c + 1) * LANES]
        acc = part if acc is None else acc + part
    w = g_col * _gelu_tanh(jnp.sum(acc, axis=-1, keepdims=True))
    outs_lo, outs_hi = [], []
    for c in range(C):
        lo, hi = _unpack_bf16_pair(chunk(C + c))
        outs_lo.append(jnp.sum(w * lo, axis=0, keepdims=True))
        outs_hi.append(jnp.sum(w * hi, axis=0, keepdims=True))
    return jnp.concatenate(outs_lo + outs_hi, axis=-1)


def _peer_kernel(eid_hbm, tab_hbm, x_ref, g_ref, o_ref, *scratch):
    R = PEER_RING
    e_bufs, bufs, rsem, esem = scratch[:2], scratch[2:2 + R], scratch[2 + R], scratch[3 + R]
    i = pl.program_id(0)
    n = pl.num_programs(0)
    T, D = x_ref.shape
    S = g_ref.shape[-1]
    C = D // (2 * LANES)
    G = T // R
    gsz = R * S

    def eid_copy(grp, par):
        return pltpu.make_async_copy(eid_hbm.at[pl.ds(grp * gsz, gsz)], e_bufs[par], esem.at[par])

    def issue(slot, par, tok):
        for k in range(S):
            e = e_bufs[par][tok * S + k]
            pltpu.async_copy(tab_hbm.at[pl.ds(e * (2 * C), 2 * C), :], bufs[slot].at[k], rsem.at[slot],
                             priority=k % 2)

    def wait(slot):
        pltpu.make_async_copy(bufs[slot], bufs[slot], rsem.at[slot]).wait()

    @pl.when(i == 0)
    def _():
        eid_copy(0, 0).start()
        eid_copy(1, 1).start()
        eid_copy(0, 0).wait()
        for t in range(R - 1):
            issue(t, 0, t)

    eye = lax.broadcasted_iota(jnp.int32, (S, S), 0) == lax.broadcasted_iota(jnp.int32, (S, S), 1)

    def evaluate(slot, t):
        buf = bufs[slot]
        groups = [jnp.swapaxes(buf[pl.ds(8 * m, 8)], 0, 1) for m in range(S // 8)]
        chunk = lambda c: jnp.concatenate([gm[c] for gm in groups], axis=0)
        o_ref[pl.ds(t, 1), :] = _expert_eval(chunk, x_ref[pl.ds(t, 1), :], g_ref[pl.ds(t, 1), :], eye)

    def group(grp, par, t0):
        for j in range(R):
            wait(j)
            if j == 0:
                issue(R - 1, par, R - 1)
                eid_copy(grp + 2, par).start()
            else:
                if j == 1:
                    eid_copy(grp + 1, 1 - par).wait()
                issue(j - 1, 1 - par, j - 1)
            evaluate(j, t0 + j)

    def body(gp, carry):
        for par in range(2):
            gi = 2 * gp + par
            group(i * G + gi, par, pl.multiple_of(gi * R, R))
        return carry

    lax.fori_loop(0, G // 2, body, 0)

    @pl.when(i == n - 1)
    def _():
        for j in range(R - 1):
            wait(j)
        eid_copy(n * G + 1, 1).wait()


def _peer_experts(eid_flat, g, h2, tab):
    ntok, D = h2.shape
    S = g.shape[-1]
    T = PEER_TOK_BLOCK
    C = D // (2 * LANES)
    assert T % (2 * PEER_RING) == 0
    return pl.pallas_call(
        _peer_kernel,
        grid=(ntok // T,),
        in_specs=[pl.BlockSpec(memory_space=pl.ANY),
                  pl.BlockSpec(memory_space=pl.ANY),
                  pl.BlockSpec((T, D), lambda i: (i, 0)),
                  pl.BlockSpec((T, S), lambda i: (i, 0))],
        out_specs=pl.BlockSpec((T, D), lambda i: (i, 0)),
        out_shape=jax.ShapeDtypeStruct((ntok, D), F32),
        scratch_shapes=[pltpu.SMEM((PEER_RING * S,), jnp.int32) for _ in range(2)]
        + [pltpu.VMEM((S, 2 * C, LANES), jnp.uint32) for _ in range(PEER_RING)]
        + [pltpu.SemaphoreType.DMA((PEER_RING,)), pltpu.SemaphoreType.DMA((2,))],
        compiler_params=pltpu.CompilerParams(dimension_semantics=("arbitrary",),
                                             vmem_limit_bytes=VMEM_LIMIT_BYTES),
        name="peer_experts",
    )(eid_flat, tab, h2, g)


def _sc_stage_rows(tab3, idx, S):
    nw, units, rows = idx.shape
    upt = S // rows
    tpw = units // upt
    c2 = tab3.shape[1]
    mesh = plsc.VectorSubcoreMesh(core_axis_name="c", subcore_axis_name="s")

    @functools.partial(
        pl.kernel, mesh=mesh,
        out_type=jax.ShapeDtypeStruct((nw * tpw, c2, S, LANES), jnp.uint32),
        scratch_types=[pltpu.VMEM((rows,), jnp.int32), pltpu.VMEM((rows, c2, LANES), jnp.uint32),
                       pltpu.SemaphoreType.DMA],
    )
    def k(tab_hbm, idx_hbm, out_hbm, idx_v, rows_v, sem):
        wid = lax.axis_index("s") * SC_CORES + lax.axis_index("c")

        def body(u, carry):
            pltpu.sync_copy(idx_hbm.at[wid, u], idx_v)
            pltpu.async_copy(tab_hbm.at[idx_v], rows_v, sem).wait()
            tok = wid * tpw + u // upt
            k0 = (u % upt) * rows
            for c in range(c2):
                pltpu.sync_copy(rows_v.at[:, c, :], out_hbm.at[tok, c, pl.ds(k0, rows), :])
            return carry

        lax.fori_loop(0, units, body, 0)

    return k(tab3, idx)


def _peer_staged_kernel(st_ref, x_ref, g_ref, o_ref):
    T = x_ref.shape[0]
    S = g_ref.shape[-1]
    eye = lax.broadcasted_iota(jnp.int32, (S, S), 0) == lax.broadcasted_iota(jnp.int32, (S, S), 1)
    for t in range(T):
        o_ref[t:t + 1, :] = _expert_eval(lambda c: st_ref[t, c], x_ref[t:t + 1, :], g_ref[t:t + 1, :], eye)


def _peer_staged(stage, g, h2):
    n, c2, S, _ = stage.shape
    D = h2.shape[-1]
    T = STAGED_TOK_BLOCK
    return pl.pallas_call(
        _peer_staged_kernel,
        grid=(n // T,),
        in_specs=[pl.BlockSpec((T, c2, S, LANES), lambda i: (i, 0, 0, 0)),
                  pl.BlockSpec((T, D), lambda i: (i, 0)),
                  pl.BlockSpec((T, S), lambda i: (i, 0))],
        out_specs=pl.BlockSpec((T, D), lambda i: (i, 0)),
        out_shape=jax.ShapeDtypeStruct((n, D), F32),
        compiler_params=pltpu.CompilerParams(dimension_semantics=("parallel",),
                                             vmem_limit_bytes=VMEM_LIMIT_BYTES),
        name="peer_staged",
    )(stage, h2, g)


def _final_kernel(x1_ref, p_ref, gt_ref, gpost_ref, o_ref):
    o_ref[0] = x1_ref[0] + gt_ref[0] * (_rms(p_ref[0]) * gpost_ref[...])


def _final(x1, peer_out, gt2, g_post):
    B, L, D = x1.shape
    tm = min(2 * PROJ_TOKENS, L)
    tok = pl.BlockSpec((1, tm, D), lambda b, i: (b, i, 0))
    return pl.pallas_call(
        _final_kernel,
        grid=(B, L // tm),
        in_specs=[tok, tok, pl.BlockSpec((1, 1, D), lambda b, i: (b, 0, 0)), _const_spec((1, D))],
        out_specs=tok,
        out_shape=jax.ShapeDtypeStruct((B, L, D), F32),
        compiler_params=pltpu.CompilerParams(dimension_semantics=("parallel", "parallel")),
        name="final",
    )(x1, peer_out, gt2, g_post.reshape(1, D))


def _pack_bf16_pairs(a):
    bits = lax.bitcast_convert_type(a.astype(jnp.bfloat16), jnp.uint16).astype(jnp.uint32)
    half = a.shape[1] // 2
    return bits[:, :half] | (bits[:, half:] << 16)


def _split_w_in(w_in, d_ssd, xbc_dim, nh2, d_conv):
    bounds = [0, d_ssd, d_ssd + xbc_dim, d_ssd + xbc_dim + nh2]
    bounds += [bounds[-1] + d_conv, bounds[-1] + 2 * d_conv, bounds[-1] + 3 * d_conv]
    wz, wxbc, wdt, wgb, wgc, wv = (w_in[:, bounds[i]:bounds[i + 1]].astype(MXU_DTYPE) for i in range(6))
    return wz, wxbc, wdt, wgb, wgc, wv


def kernel(x, c, ctx, c_ctx, w_ada, b_ada, g_pre_mix, g_post_mix, g_pre_ffn, g_post_ffn, w_in, ssd_conv_w, ssd_conv_b, ssd_dt_bias, ssd_a_log, ssd_d, ssd_norm_g, sc_conv_w, sc_norm_g, w_out, peer_w_q, peer_keys, peer_u, peer_v):
    B, L, D = x.shape
    ctx_len = ctx.shape[1]
    assert w_ada.shape[0] == 1, "single-layer block"
    nh2 = ssd_dt_bias.shape[1] * ssd_dt_bias.shape[2]
    nh = nh2 // 2
    d_ssd = nh * SSD_HEAD_DIM
    xbc_dim = ssd_conv_w.shape[-1]
    d_conv = sc_conv_w.shape[-1]
    n_mod = w_ada.shape[-1] // D

    pad_rows = (-(B + 1)) % 8
    cc = jnp.concatenate([c, c_ctx[None, :], jnp.zeros((pad_rows, D), F32)], axis=0)
    mod = _adaln(cc, w_ada[0], b_ada[0])
    mod_l = mod[:B].reshape(B, n_mod, 1, D)
    mod_c = jnp.broadcast_to(mod[B].reshape(1, n_mod, 1, D), (B, n_mod, 1, D))
    sh1_l, sc1_l, gt1_l, sh2_l, sc2_l, gt2_l = (mod_l[:, j] for j in range(n_mod))
    sh1_c, sc1_c = mod_c[:, 0], mod_c[:, 1]

    w = _split_w_in(w_in[0], d_ssd, xbc_dim, nh2, d_conv)
    proj_args = (w, ssd_conv_w[0], ssd_conv_b[0], ssd_dt_bias[0].reshape(nh2), sc_conv_w[0], sc_norm_g[0])
    a_log = ssd_a_log[0]

    _, xbc_c, dt_c, dtT_c, _ = _in_proj(ctx, sh1_c, sc1_c, g_pre_mix[0], *proj_args, row_w=ctx_len)
    gn = SSD_GROUPS * SSD_STATE
    h0 = jnp.zeros((B, 2, gn, d_ssd // SSD_GROUPS), F32)
    _, h_ctx = _ssd_scan(xbc_c, dt_c, dtT_c, a_log, h0)

    z_l, xbc_l, dt_l, dtT_l, sco_l = _in_proj(x, sh1_l, sc1_l, g_pre_mix[0], *proj_args, row_w=GRID_W)
    y2, _ = _ssd_scan(xbc_l, dt_l, dtT_l, a_log, h_ctx)

    wo = w_out[0].astype(MXU_DTYPE)
    d_skip_e = jnp.repeat(ssd_d[0], SSD_HEAD_DIM)
    x1, h2 = _mix_out(x, y2, xbc_l, z_l, sco_l, d_skip_e, ssd_norm_g[0], wo[:d_ssd], wo[d_ssd:],
                      g_post_mix[0], gt1_l, g_pre_ffn[0], sh2_l, sc2_l)

    ntok = B * L
    qd = peer_w_q.shape[-1] // PEER_HEADS
    wq = peer_w_q[0].reshape(D, PEER_HEADS, qd).transpose(1, 0, 2).astype(MXU_DTYPE)
    eidT, gT = _peer_route(h2.reshape(ntok, D), wq, peer_keys[0].astype(MXU_DTYPE))
    tab = jnp.concatenate([_pack_bf16_pairs(peer_u[0]), _pack_bf16_pairs(peer_v[0])], axis=1).reshape(-1, LANES)
    slots = eidT.shape[0]
    eid_tok = eidT.T
    g_tok = gT.T
    h2f = h2.reshape(ntok, D)

    n_sc = SC_TOKEN_SHARE
    n_tc = ntok - n_sc
    nw = SC_CORES * SC_SUBCORES
    assert n_sc % (nw * STAGED_TOK_BLOCK) == 0 and n_tc % PEER_TOK_BLOCK == 0 and slots % SC_ROWS == 0
    sc_idx = eid_tok[n_tc:].reshape(nw, (n_sc // nw) * (slots // SC_ROWS), SC_ROWS)
    stage = _sc_stage_rows(tab.reshape(-1, tab.shape[0] // peer_u.shape[1], LANES), sc_idx, slots)

    eid = jnp.pad(eid_tok[:n_tc].reshape(-1), (0, 2 * PEER_RING * slots))
    out_tc = _peer_experts(eid, g_tok[:n_tc], h2f[:n_tc], tab)
    out_sc = _peer_staged(stage, g_tok[n_tc:], h2f[n_tc:])
    peer_out = jnp.concatenate([out_tc, out_sc], axis=0)

    return _final(x1, peer_out.reshape(B, L, D), gt2_l, g_post_ffn[0])
```

```python
import functools

import jax
import jax.numpy as jnp
import numpy as np
from jax import lax
from jax.experimental import pallas as pl
from jax.experimental.pallas import tpu as pltpu

GRID_W = 64
SSD_HEAD_DIM = 64
SSD_GROUPS = 4
SSD_STATE = 128
SSD_CHUNK = 128
PEER_HEADS = 8
PEER_N_KEYS = 128
PEER_TOPK = 16
EPS = 1e-6
LANES = 128

MXU_DTYPE = jnp.bfloat16
VMEM_LIMIT_BYTES = 56 * 1024 * 1024

PROJ_TOKENS = 256
ROUTE_TOKENS = 1024
PEER_TOK_BLOCK = 128
PEER_RING = 8

F32 = jnp.float32


def _dot(a, b):
    return jnp.dot(a, b, preferred_element_type=F32)


def _dot_nt(a, b):
    return lax.dot_general(a, b, (((1,), (1,)), ((), ())), preferred_element_type=F32)


def _dot_tn(a, b):
    return lax.dot_general(a, b, (((0,), (0,)), ((), ())), preferred_element_type=F32)


def _split_bf16(a, terms):
    parts = []
    r = a
    for i in range(terms):
        p = r.astype(MXU_DTYPE)
        parts.append(p)
        if i + 1 < terms:
            r = r - p.astype(F32)
    return parts


def _silu(x):
    return x * jax.nn.sigmoid(x)


def _softplus(x):
    return jnp.maximum(x, 0.0) + jnp.log1p(jnp.exp(-jnp.abs(x)))


def _gelu_tanh(x):
    return 0.5 * x * (1.0 + jnp.tanh(0.7978845608028654 * (x + 0.044715 * (x * x * x))))


def _rms(x):
    return x * lax.rsqrt(jnp.mean(x * x, axis=-1, keepdims=True) + EPS)


def _const_spec(shape):
    zeros = (0,) * len(shape)
    return pl.BlockSpec(shape, lambda *_: zeros, pipeline_mode=pl.Buffered(1))


def _adaln_kernel(c_ref, w_ref, b_ref, o_ref):
    s = _silu(c_ref[...]).astype(MXU_DTYPE)
    o_ref[...] = _dot(s, w_ref[...].astype(MXU_DTYPE)) + b_ref[...]


def _adaln(cc, w, b):
    rows, d = cc.shape
    n = w.shape[1]
    bn = 1024
    return pl.pallas_call(
        _adaln_kernel,
        grid=(n // bn,),
        in_specs=[pl.BlockSpec((rows, d), lambda j: (0, 0)),
                  pl.BlockSpec((d, bn), lambda j: (0, j)),
                  pl.BlockSpec((1, bn), lambda j: (0, j))],
        out_specs=pl.BlockSpec((rows, bn), lambda j: (0, j)),
        out_shape=jax.ShapeDtypeStruct((rows, n), F32),
        name="adaln",
    )(cc, w, b.reshape(1, n))


def _dwconv(u, w, pos, row_w):
    taps = w.shape[0]
    p = taps // 2
    n = u.shape[0]
    out = u * w[p:p + 1, :]
    for k in range(taps):
        d = k - p
        if d == 0:
            continue
        shifted = pltpu.roll(u, (-d) % n, axis=0)
        valid = jnp.logical_and(pos + d >= 0, pos + d < row_w)
        out = out + jnp.where(valid, shifted, 0.0) * w[k:k + 1, :]
    return out


def _in_proj_kernel(x_ref, sh_ref, sc_ref, gpre_ref, wz_ref, wxbc_ref, wdt_ref, wdtT_ref, wgb_ref, wgc_ref,
                    wv_ref, cw_ref, cb_ref, dtb_ref, dtbT_ref, scw_ref, scg_ref,
                    z_ref, xbc_ref, dt_ref, dtT_ref, sco_ref, *, row_w):
    x = x_ref[0]
    tm = x.shape[0]
    h = _rms(x) * gpre_ref[...]
    h = h * (1.0 + sc_ref[0]) + sh_ref[0]
    hb = h.astype(MXU_DTYPE)
    pos = lax.broadcasted_iota(jnp.int32, (tm, 1), 0) % row_w

    z_ref[0] = _dot(hb, wz_ref[...])
    xbc = _dwconv(_dot(hb, wxbc_ref[...]), cw_ref[...], pos, row_w) + cb_ref[...]
    xbc_ref[0] = _silu(xbc)

    nh = dt_ref.shape[-1]
    dt = _softplus(_dot(hb, wdt_ref[...]) + dtb_ref[...])
    dt_ref[0, 0] = dt[:, :nh]
    dt_ref[1, 0] = dt[:, nh:]
    dtT = _softplus(_dot_nt(wdtT_ref[...], hb) + dtbT_ref[...])
    dtT_ref[0, 0] = dtT[:nh]
    dtT_ref[1, 0] = dtT[nh:]

    gcv = _dot(hb, wgc_ref[...]) * _dot(hb, wv_ref[...])
    sc = _dot(hb, wgb_ref[...]) * _dwconv(gcv, scw_ref[...], pos, row_w)
    sco_ref[0] = _rms(sc) * scg_ref[...]


def _in_proj(x, sh, sc, g_pre, w, conv_w, conv_b, dt_bias, sc_conv_w, sc_norm_g, row_w):
    B, L, D = x.shape
    tm = min(PROJ_TOKENS, L)
    assert L % tm == 0 and tm % row_w == 0
    wz, wxbc, wdt, wgb, wgc, wv = w
    d_ssd, xbc_dim, nh2, d_conv = wz.shape[1], wxbc.shape[1], wdt.shape[1], wgb.shape[1]
    nh = nh2 // 2
    tok = lambda cols: pl.BlockSpec((1, tm, cols), lambda b, i: (b, i, 0))
    per_b = pl.BlockSpec((1, 1, D), lambda b, i: (b, 0, 0))
    return pl.pallas_call(
        functools.partial(_in_proj_kernel, row_w=row_w),
        grid=(B, L // tm),
        in_specs=[tok(D), per_b, per_b, _const_spec((1, D)),
                  _const_spec(wz.shape), _const_spec(wxbc.shape), _const_spec(wdt.shape),
                  _const_spec((nh2, D)), _const_spec(wgb.shape), _const_spec(wgc.shape), _const_spec(wv.shape),
                  _const_spec(conv_w.shape), _const_spec((1, xbc_dim)), _const_spec((1, nh2)),
                  _const_spec((nh2, 1)), _const_spec(sc_conv_w.shape), _const_spec((1, d_conv))],
        out_specs=[tok(d_ssd), tok(xbc_dim),
                   pl.BlockSpec((2, 1, tm, nh), lambda b, i: (0, b, i, 0)),
                   pl.BlockSpec((2, 1, nh, tm), lambda b, i: (0, b, 0, i)),
                   tok(d_conv)],
        out_shape=[jax.ShapeDtypeStruct((B, L, d_ssd), F32),
                   jax.ShapeDtypeStruct((B, L, xbc_dim), F32),
                   jax.ShapeDtypeStruct((2, B, L, nh), F32),
                   jax.ShapeDtypeStruct((2, B, nh, L), F32),
                   jax.ShapeDtypeStruct((B, L, d_conv), F32)],
        compiler_params=pltpu.CompilerParams(dimension_semantics=("parallel", "parallel"),
                                             vmem_limit_bytes=VMEM_LIMIT_BYTES),
        name="in_proj",
    )(x, sh, sc, g_pre.reshape(1, D), wz, wxbc, wdt, wdt.T, wgb, wgc, wv,
      conv_w, conv_b.reshape(1, xbc_dim), dt_bias.reshape(1, nh2), dt_bias.reshape(nh2, 1),
      sc_conv_w, sc_norm_g.reshape(1, d_conv))


def _ssd_kernel(xs_ref, bm_ref, cm_ref, dt_ref, dtT_ref, alog_ref, alogT_ref, h0_ref, y_ref, hf_ref, st_ref):
    d = pl.program_id(1)
    c = pl.program_id(2)
    nc = pl.num_programs(2)
    Q = SSD_CHUNK
    N = SSD_STATE
    P = SSD_HEAD_DIM
    d_ssd = xs_ref.shape[-1]
    nh = dt_ref.shape[-1]
    hpg = nh // SSD_GROUPS
    gw = hpg * P

    @pl.when(c == 0)
    def _():
        st_ref[...] = h0_ref[0, 0]

    a = dt_ref[0, 0] * (-jnp.exp(alog_ref[0]))
    aT = dtT_ref[0, 0] * (-jnp.exp(alogT_ref[0]))
    row = lax.broadcasted_iota(jnp.int32, (Q, Q), 0)
    col = lax.broadcasted_iota(jnp.int32, (Q, Q), 1)
    fwd = d == 0
    mask = (row - col) * (1 - 2 * d) >= 0
    maskb = mask.astype(F32).astype(MXU_DTYPE)
    acs = sum(_dot(maskb, p) for p in _split_bf16(a, 3))
    acsT = sum(_dot_nt(p, maskb) for p in _split_bf16(aT, 3))

    erow = lax.broadcasted_iota(jnp.int32, (nh, d_ssd), 0)
    ecol = lax.broadcasted_iota(jnp.int32, (nh, d_ssd), 1)
    expand = (ecol // P == erow).astype(F32).astype(MXU_DTYPE)
    acs_e = sum(_dot(p, expand) for p in _split_bf16(acs, 2))
    dt_e = sum(_dot(p, expand) for p in _split_bf16(dt_ref[0, 0], 2))
    tot_e = jnp.where(fwd, acs_e[Q - 1:Q, :], acs_e[0:1, :])

    xdt = xs_ref[0] * dt_e
    xw = (xdt * jnp.exp(tot_e - acs_e)).astype(MXU_DTYPE)
    xdtb = xdt.astype(MXU_DTYPE)
    eacs = jnp.exp(acs_e)
    cdec = jnp.exp(tot_e)
    lane = lax.broadcasted_iota(jnp.int32, (Q, 2 * P), 1)

    for g in range(SSD_GROUPS):
        bm = bm_ref[0, :, g * N:(g + 1) * N].astype(MXU_DTYPE)
        cm = cm_ref[0, :, g * N:(g + 1) * N].astype(MXU_DTYPE)
        cb = _dot_nt(cm, bm)
        hT = st_ref[g * N:(g + 1) * N, :]
        gcols = slice(g * gw, (g + 1) * gw)
        y_off = _dot(cm, hT.astype(MXU_DTYPE)) * eacs[:, gcols]
        st_ref[g * N:(g + 1) * N, :] = hT * cdec[:, gcols] + _dot_tn(bm, xw[:, gcols])
        for pr in range(hpg // 2):
            pcols = slice(g * gw + pr * 2 * P, g * gw + (pr + 1) * 2 * P)
            xpair = xdtb[:, pcols]
            halves = []
            for r in range(2):
                hd = g * hpg + pr * 2 + r
                seg = acs[:, hd:hd + 1] - acsT[hd:hd + 1, :]
                decay = jnp.exp(jnp.where(mask, seg, -jnp.inf))
                halves.append(_dot((cb * decay).astype(MXU_DTYPE), xpair))
            y_diag = jnp.where(lane < P, halves[0], halves[1])
            y_ref[0, 0, :, pcols] = y_diag + y_off[:, pr * 2 * P:(pr + 1) * 2 * P]

    @pl.when(c == nc - 1)
    def _():
        hf_ref[0, 0] = st_ref[...]


def _ssd_scan(xbc, dt, dtT, a_log, h0):
    B, L, _ = xbc.shape
    nh = dt.shape[-1]
    d_ssd = nh * SSD_HEAD_DIM
    gn = SSD_GROUPS * SSD_STATE
    Q = SSD_CHUNK
    nc = L // Q
    gw = d_ssd // SSD_GROUPS
    chunk = lambda d, c: c + d * (nc - 1 - 2 * c)
    return pl.pallas_call(
        _ssd_kernel,
        grid=(B, 2, nc),
        in_specs=[pl.BlockSpec((1, Q, d_ssd), lambda b, d, c: (b, chunk(d, c), 0)),
                  pl.BlockSpec((1, Q, gn), lambda b, d, c: (b, chunk(d, c), d_ssd // gn)),
                  pl.BlockSpec((1, Q, gn), lambda b, d, c: (b, chunk(d, c), d_ssd // gn + 1)),
                  pl.BlockSpec((1, 1, Q, nh), lambda b, d, c: (d, b, chunk(d, c), 0)),
                  pl.BlockSpec((1, 1, nh, Q), lambda b, d, c: (d, b, 0, chunk(d, c))),
                  pl.BlockSpec((1, 1, nh), lambda b, d, c: (d, 0, 0)),
                  pl.BlockSpec((1, nh, 1), lambda b, d, c: (d, 0, 0)),
                  pl.BlockSpec((1, 1, gn, gw), lambda b, d, c: (b, d, 0, 0))],
        out_specs=[pl.BlockSpec((1, 1, Q, d_ssd), lambda b, d, c: (d, b, chunk(d, c), 0)),
                   pl.BlockSpec((1, 1, gn, gw), lambda b, d, c: (b, d, 0, 0))],
        out_shape=[jax.ShapeDtypeStruct((2, B, L, d_ssd), F32),
                   jax.ShapeDtypeStruct((B, 2, gn, gw), F32)],
        scratch_shapes=[pltpu.VMEM((gn, gw), F32)],
        compiler_params=pltpu.CompilerParams(dimension_semantics=("parallel", "arbitrary", "arbitrary"),
                                             vmem_limit_bytes=VMEM_LIMIT_BYTES),
        name="ssd_scan",
    )(xbc, xbc, xbc, dt, dtT, a_log.reshape(2, 1, nh), a_log.reshape(2, nh, 1), h0)


def _mix_out_kernel(x_ref, yf_ref, yb_ref, xs_ref, z_ref, sco_ref, dsk_ref, ng_ref, wo1_ref, wo2_ref,
                    gpost_ref, gt_ref, gpre_ref, sh_ref, sc_ref, x1_ref, h2_ref):
    d_ssd = xs_ref.shape[-1]
    gw = d_ssd // SSD_GROUPS
    y = yf_ref[0, 0] + yb_ref[0, 0] + dsk_ref[...] * xs_ref[0]
    y = y * _silu(z_ref[0])
    parts = [_rms(y[:, g * gw:(g + 1) * gw]) for g in range(SSD_GROUPS)]
    yn = jnp.concatenate(parts, axis=-1) * ng_ref[...]
    mix = _dot(yn.astype(MXU_DTYPE), wo1_ref[...]) + _dot(sco_ref[0].astype(MXU_DTYPE), wo2_ref[...])
    x1 = x_ref[0] + gt_ref[0] * (_rms(mix) * gpost_ref[...])
    x1_ref[0] = x1
    h2 = _rms(x1) * gpre_ref[...]
    h2_ref[0] = h2 * (1.0 + sc_ref[0]) + sh_ref[0]


def _mix_out(x, y2, xbc, z, sco, d_skip_e, norm_g, wo1, wo2, g_post, gt1, g_pre_ffn, sh2, sc2):
    B, L, D = x.shape
    d_ssd = z.shape[-1]
    d_conv = sco.shape[-1]
    tm = min(PROJ_TOKENS, L)
    tok = lambda cols: pl.BlockSpec((1, tm, cols), lambda b, i: (b, i, 0))
    per_b = pl.BlockSpec((1, 1, D), lambda b, i: (b, 0, 0))
    ydir = lambda d: pl.BlockSpec((1, 1, tm, d_ssd), lambda b, i: (d, b, i, 0))
    return pl.pallas_call(
        _mix_out_kernel,
        grid=(B, L // tm),
        in_specs=[tok(D), ydir(0), ydir(1), tok(d_ssd), tok(d_ssd), tok(d_conv),
                  _const_spec((1, d_ssd)), _const_spec((1, d_ssd)), _const_spec(wo1.shape), _const_spec(wo2.shape),
                  _const_spec((1, D)), per_b, _const_spec((1, D)), per_b, per_b],
        out_specs=[tok(D), tok(D)],
        out_shape=[jax.ShapeDtypeStruct((B, L, D), F32), jax.ShapeDtypeStruct((B, L, D), F32)],
        compiler_params=pltpu.CompilerParams(dimension_semantics=("parallel", "parallel"),
                                             vmem_limit_bytes=VMEM_LIMIT_BYTES),
        name="mix_out",
    )(x, y2, y2, xbc, z, sco, d_skip_e.reshape(1, d_ssd), norm_g.reshape(1, d_ssd), wo1, wo2,
      g_post.reshape(1, D), gt1, g_pre_ffn.reshape(1, D), sh2, sc2)


def _topk_rows(s, k):
    n = s.shape[0]
    rows = lax.broadcasted_iota(jnp.int32, s.shape, 0).astype(F32)
    vals, idxs = [], []
    for _ in range(k):
        m = jnp.max(s, axis=0, keepdims=True)
        i = jnp.min(jnp.where(s == m, rows, float(n)), axis=0, keepdims=True)
        vals.append(m)
        idxs.append(i)
        s = jnp.where(rows == i, -jnp.inf, s)
    return jnp.concatenate(vals, axis=0), jnp.concatenate(idxs, axis=0)


def _pair_candidates(k):
    return [(i, j) for i in range(k) for j in range(k) if (i + 1) * (j + 1) <= k]


def _pair_select_matrices(k):
    pairs = _pair_candidates(k)
    rows = -(-len(pairs) // 8) * 8
    sel0 = np.zeros((rows, k), np.float32)
    sel1 = np.zeros((rows, k), np.float32)
    bias = np.full((rows, 1), -np.inf, np.float32)
    for r, (i, j) in enumerate(pairs):
        sel0[r, i] = 1.0
        sel1[r, j] = 1.0
        bias[r, 0] = 0.0
    return sel0, sel1, bias


def _select_rows(sel, x):
    p = _split_bf16(x, 3)
    return (_dot(sel, p[0]) + _dot(sel, p[1])) + _dot(sel, p[2])


def _route_kernel(h2_ref, wq_ref, keys_ref, sel0_ref, sel1_ref, bias_ref, eid_ref, g_ref, eid_s, g_s):
    K = PEER_TOPK
    NK = PEER_N_KEYS
    hb = h2_ref[...].astype(MXU_DTYPE)
    T = hb.shape[0]
    half = keys_ref.shape[-1]
    sel0 = sel0_ref[...]
    sel1 = sel1_ref[...]
    n_cand = sel0.shape[0]
    c_rows = lax.broadcasted_iota(jnp.int32, (n_cand, T), 0).astype(F32)

    def head(h, carry):
        q = _dot(hb, wq_ref[h]).astype(MXU_DTYPE)
        sv, si = [], []
        for p in range(2):
            s = _dot_nt(keys_ref[h, p], q[:, p * half:(p + 1) * half])
            v, i = _topk_rows(s, K)
            sv.append(v)
            si.append(i)
        cand = _select_rows(sel0, sv[0]) + _select_rows(sel1, sv[1]) + bias_ref[...]
        ecand = _dot(sel0, si[0].astype(MXU_DTYPE)) * float(NK) + _dot(sel1, si[1].astype(MXU_DTYPE))
        cv, eid = [], []
        for _ in range(K):
            m = jnp.max(cand, axis=0, keepdims=True)
            i = jnp.min(jnp.where(cand == m, c_rows, float(n_cand)), axis=0, keepdims=True)
            hit = c_rows == i
            cv.append(m)
            eid.append(jnp.max(jnp.where(hit, ecand, -1.0), axis=0, keepdims=True))
            cand = jnp.where(hit, -jnp.inf, cand)
        cvs = jnp.concatenate(cv, axis=0)
        ex = jnp.exp(cvs - cv[0])
        gate = ex / jnp.sum(ex, axis=0, keepdims=True)
        rows = pl.ds(pl.multiple_of(h * K, K), K)
        eid_s[rows, :] = jnp.concatenate(eid, axis=0)
        g_s[rows, :] = gate
        return carry

    lax.fori_loop(0, PEER_HEADS, head, 0)
    eid_ref[...] = eid_s[...].T.astype(jnp.int32)
    g_ref[...] = g_s[...].T


def _peer_route(h2, wq, keys):
    ntok, D = h2.shape
    T = ROUTE_TOKENS
    slots = PEER_HEADS * PEER_TOPK
    sel0, sel1, bias = _pair_select_matrices(PEER_TOPK)
    sel0 = jnp.asarray(sel0, MXU_DTYPE)
    sel1 = jnp.asarray(sel1, MXU_DTYPE)
    return pl.pallas_call(
        _route_kernel,
        grid=(ntok // T,),
        in_specs=[pl.BlockSpec((T, D), lambda i: (i, 0)), _const_spec(wq.shape), _const_spec(keys.shape),
                  _const_spec(sel0.shape), _const_spec(sel1.shape), _const_spec(bias.shape)],
        out_specs=[pl.BlockSpec((T, slots), lambda i: (i, 0)), pl.BlockSpec((T, slots), lambda i: (i, 0))],
        out_shape=[jax.ShapeDtypeStruct((ntok, slots), jnp.int32), jax.ShapeDtypeStruct((ntok, slots), F32)],
        scratch_shapes=[pltpu.VMEM((slots, T), F32), pltpu.VMEM((slots, T), F32)],
        compiler_params=pltpu.CompilerParams(dimension_semantics=("parallel",),
                                             vmem_limit_bytes=VMEM_LIMIT_BYTES),
        name="peer_route",
    )(h2, wq, keys, sel0, sel1, jnp.asarray(bias))


def _unpack_bf16_pair(w):
    lo = lax.bitcast_convert_type(w << 16, F32)
    hi = lax.bitcast_convert_type(w & jnp.uint32(0xFFFF0000), F32)
    return lo, hi


def _expert_eval(chunk, x_row, g_row, eye):
    D = x_row.shape[-1]
    C = D // (2 * LANES)
    g_col = jnp.sum(jnp.where(eye, g_row, 0.0), axis=-1, keepdims=True)
    acc = None
    for c in range(C):
        lo, hi = _unpack_bf16_pair(chunk(c))
        part = lo * x_row[:, c * LANES:(c + 1) * LANES] + hi * x_row[:, D // 2 + c * LANES:D // 2 + (c + 1) * LANES]
        acc = part if acc is None else acc + part
    w = g_col * _gelu_tanh(jnp.sum(acc, axis=-1, keepdims=True))
    outs_lo, outs_hi = [], []
    for c in range(C):
        lo, hi = _unpack_bf16_pair(chunk(C + c))
        outs_lo.append(jnp.sum(w * lo, axis=0, keepdims=True))
        outs_hi.append(jnp.sum(w * hi, axis=0, keepdims=True))
    return jnp.concatenate(outs_lo + outs_hi, axis=-1)


def _peer_kernel(eid_hbm, tab_hbm, x_ref, g_ref, x1_ref, gt_ref, gpost_ref, o_ref, *scratch):
    R = PEER_RING
    e_bufs, bufs, rsem, esem, peer_s = scratch[:2], scratch[2:2 + R], scratch[2 + R], scratch[3 + R], scratch[4 + R]
    i = pl.program_id(0)
    n = pl.num_programs(0)
    T, D = x_ref.shape
    S = g_ref.shape[-1]
    C = D // (2 * LANES)
    G = T // R
    gsz = R * S

    def eid_copy(grp, par):
        return pltpu.make_async_copy(eid_hbm.at[pl.ds(grp * gsz, gsz)], e_bufs[par], esem.at[par])

    def issue(slot, par, tok):
        for k in range(S):
            e = e_bufs[par][tok * S + k]
            pltpu.async_copy(tab_hbm.at[pl.ds(e * (2 * C), 2 * C), :], bufs[slot].at[k], rsem.at[slot],
                             priority=k % 2)

    def wait(slot):
        pltpu.make_async_copy(bufs[slot], bufs[slot], rsem.at[slot]).wait()

    @pl.when(i == 0)
    def _():
        eid_copy(0, 0).start()
        eid_copy(1, 1).start()
        eid_copy(0, 0).wait()
        for t in range(R - 1):
            issue(t, 0, t)

    eye = lax.broadcasted_iota(jnp.int32, (S, S), 0) == lax.broadcasted_iota(jnp.int32, (S, S), 1)

    def evaluate(slot, t):
        buf = bufs[slot]
        groups = [jnp.swapaxes(buf[pl.ds(8 * m, 8)], 0, 1) for m in range(S // 8)]
        chunk = lambda c: jnp.concatenate([gm[c] for gm in groups], axis=0)
        peer_s[pl.ds(t, 1), :] = _expert_eval(chunk, x_ref[pl.ds(t, 1), :], g_ref[pl.ds(t, 1), :], eye)

    def group(grp, par, t0):
        for j in range(R):
            wait(j)
            if j == 0:
                issue(R - 1, par, R - 1)
                eid_copy(grp + 2, par).start()
            else:
                if j == 1:
                    eid_copy(grp + 1, 1 - par).wait()
                issue(j - 1, 1 - par, j - 1)
            evaluate(j, t0 + j)

    def body(gp, carry):
        for par in range(2):
            gi = 2 * gp + par
            group(i * G + gi, par, pl.multiple_of(gi * R, R))
        return carry

    lax.fori_loop(0, G // 2, body, 0)
    o_ref[...] = x1_ref[...] + gt_ref[0] * (_rms(peer_s[...]) * gpost_ref[...])

    @pl.when(i == n - 1)
    def _():
        for j in range(R - 1):
            wait(j)
        eid_copy(n * G + 1, 1).wait()


def _peer_experts(eid_flat, g, h2, tab, x1, gt2, g_post):
    ntok, D = h2.shape
    S = g.shape[-1]
    T = PEER_TOK_BLOCK
    C = D // (2 * LANES)
    steps_per_batch = ntok // gt2.shape[0] // T
    assert T % (2 * PEER_RING) == 0 and ntok == steps_per_batch * T * gt2.shape[0]
    return pl.pallas_call(
        _peer_kernel,
        grid=(ntok // T,),
        in_specs=[pl.BlockSpec(memory_space=pl.ANY),
                  pl.BlockSpec(memory_space=pl.ANY),
                  pl.BlockSpec((T, D), lambda i: (i, 0)),
                  pl.BlockSpec((T, S), lambda i: (i, 0)),
                  pl.BlockSpec((T, D), lambda i: (i, 0)),
                  pl.BlockSpec((1, 1, D), lambda i: (i // steps_per_batch, 0, 0)),
                  pl.BlockSpec((1, D), lambda i: (0, 0))],
        out_specs=pl.BlockSpec((T, D), lambda i: (i, 0)),
        out_shape=jax.ShapeDtypeStruct((ntok, D), F32),
        scratch_shapes=[pltpu.SMEM((PEER_RING * S,), jnp.int32) for _ in range(2)]
        + [pltpu.VMEM((S, 2 * C, LANES), jnp.uint32) for _ in range(PEER_RING)]
        + [pltpu.SemaphoreType.DMA((PEER_RING,)), pltpu.SemaphoreType.DMA((2,)), pltpu.VMEM((T, D), F32)],
        compiler_params=pltpu.CompilerParams(dimension_semantics=("arbitrary",),
                                             vmem_limit_bytes=VMEM_LIMIT_BYTES),
        name="peer_experts",
    )(eid_flat, tab, h2, g, x1, gt2, g_post.reshape(1, D))


def _pack_bf16_pairs(a):
    bits = lax.bitcast_convert_type(a.astype(jnp.bfloat16), jnp.uint16).astype(jnp.uint32)
    half = a.shape[1] // 2
    return bits[:, :half] | (bits[:, half:] << 16)


def _split_w_in(w_in, d_ssd, xbc_dim, nh2, d_conv):
    bounds = [0, d_ssd, d_ssd + xbc_dim, d_ssd + xbc_dim + nh2]
    bounds += [bounds[-1] + d_conv, bounds[-1] + 2 * d_conv, bounds[-1] + 3 * d_conv]
    wz, wxbc, wdt, wgb, wgc, wv = (w_in[:, bounds[i]:bounds[i + 1]].astype(MXU_DTYPE) for i in range(6))
    return wz, wxbc, wdt, wgb, wgc, wv


def kernel(x, c, ctx, c_ctx, w_ada, b_ada, g_pre_mix, g_post_mix, g_pre_ffn, g_post_ffn, w_in, ssd_conv_w, ssd_conv_b, ssd_dt_bias, ssd_a_log, ssd_d, ssd_norm_g, sc_conv_w, sc_norm_g, w_out, peer_w_q, peer_keys, peer_u, peer_v):
    B, L, D = x.shape
    ctx_len = ctx.shape[1]
    assert w_ada.shape[0] == 1, "single-layer block"
    nh2 = ssd_dt_bias.shape[1] * ssd_dt_bias.shape[2]
    nh = nh2 // 2
    d_ssd = nh * SSD_HEAD_DIM
    xbc_dim = ssd_conv_w.shape[-1]
    d_conv = sc_conv_w.shape[-1]
    n_mod = w_ada.shape[-1] // D

    pad_rows = (-(B + 1)) % 8
    cc = jnp.concatenate([c, c_ctx[None, :], jnp.zeros((pad_rows, D), F32)], axis=0)
    mod = _adaln(cc, w_ada[0], b_ada[0])
    mod_l = mod[:B].reshape(B, n_mod, 1, D)
    mod_c = jnp.broadcast_to(mod[B].reshape(1, n_mod, 1, D), (B, n_mod, 1, D))
    sh1_l, sc1_l, gt1_l, sh2_l, sc2_l, gt2_l = (mod_l[:, j] for j in range(n_mod))
    sh1_c, sc1_c = mod_c[:, 0], mod_c[:, 1]

    w = _split_w_in(w_in[0], d_ssd, xbc_dim, nh2, d_conv)
    proj_args = (w, ssd_conv_w[0], ssd_conv_b[0], ssd_dt_bias[0].reshape(nh2), sc_conv_w[0], sc_norm_g[0])
    a_log = ssd_a_log[0]

    _, xbc_c, dt_c, dtT_c, _ = _in_proj(ctx, sh1_c, sc1_c, g_pre_mix[0], *proj_args, row_w=ctx_len)
    gn = SSD_GROUPS * SSD_STATE
    h0 = jnp.zeros((B, 2, gn, d_ssd // SSD_GROUPS), F32)
    _, h_ctx = _ssd_scan(xbc_c, dt_c, dtT_c, a_log, h0)

    z_l, xbc_l, dt_l, dtT_l, sco_l = _in_proj(x, sh1_l, sc1_l, g_pre_mix[0], *proj_args, row_w=GRID_W)
    y2, _ = _ssd_scan(xbc_l, dt_l, dtT_l, a_log, h_ctx)

    wo = w_out[0].astype(MXU_DTYPE)
    d_skip_e = jnp.repeat(ssd_d[0], SSD_HEAD_DIM)
    x1, h2 = _mix_out(x, y2, xbc_l, z_l, sco_l, d_skip_e, ssd_norm_g[0], wo[:d_ssd], wo[d_ssd:],
                      g_post_mix[0], gt1_l, g_pre_ffn[0], sh2_l, sc2_l)

    ntok = B * L
    qd = peer_w_q.shape[-1] // PEER_HEADS
    wq = peer_w_q[0].reshape(D, PEER_HEADS, qd).transpose(1, 0, 2).astype(MXU_DTYPE)
    eid_tok, g_tok = _peer_route(h2.reshape(ntok, D), wq, peer_keys[0].astype(MXU_DTYPE))
    tab = jnp.concatenate([_pack_bf16_pairs(peer_u[0]), _pack_bf16_pairs(peer_v[0])], axis=1).reshape(-1, LANES)
    eid = jnp.pad(eid_tok.reshape(-1), (0, 2 * PEER_RING * eid_tok.shape[1]))
    out = _peer_experts(eid, g_tok, h2.reshape(ntok, D), tab, x1.reshape(ntok, D), gt2_l, g_post_ffn[0])
    return out.reshape(B, L, D)
```

```python
import functools

import jax
import jax.numpy as jnp
import numpy as np
from jax import lax
from jax.experimental import pallas as pl
from jax.experimental.pallas import tpu as pltpu

GRID_W = 64
SSD_HEAD_DIM = 64
SSD_GROUPS = 4
SSD_STATE = 128
SSD_CHUNK = 128
PEER_HEADS = 8
PEER_N_KEYS = 128
PEER_TOPK = 16
EPS = 1e-6
LANES = 128

MXU_DTYPE = jnp.bfloat16
VMEM_LIMIT_BYTES = 56 * 1024 * 1024

PROJ_TOKENS = 256
ROUTE_TOKENS = 1024
PEER_TOK_BLOCK = 128
PEER_RING = 8

F32 = jnp.float32


def _dot(a, b):
    return jnp.dot(a, b, preferred_element_type=F32)


def _dot_nt(a, b):
    return lax.dot_general(a, b, (((1,), (1,)), ((), ())), preferred_element_type=F32)


def _dot_tn(a, b):
    return lax.dot_general(a, b, (((0,), (0,)), ((), ())), preferred_element_type=F32)


def _split_bf16(a, terms):
    parts = []
    r = a
    for i in range(terms):
        p = r.astype(MXU_DTYPE)
        parts.append(p)
        if i + 1 < terms:
            r = r - p.astype(F32)
    return parts


def _silu(x):
    return x * jax.nn.sigmoid(x)


def _softplus(x):
    return jnp.maximum(x, 0.0) + jnp.log1p(jnp.exp(-jnp.abs(x)))


def _gelu_tanh(x):
    return 0.5 * x * (1.0 + jnp.tanh(0.7978845608028654 * (x + 0.044715 * (x * x * x))))


def _rms(x):
    return x * lax.rsqrt(jnp.mean(x * x, axis=-1, keepdims=True) + EPS)


def _const_spec(shape):
    zeros = (0,) * len(shape)
    return pl.BlockSpec(shape, lambda *_: zeros, pipeline_mode=pl.Buffered(1))


def _adaln_kernel(c_ref, w_ref, b_ref, o_ref):
    s = _silu(c_ref[...]).astype(MXU_DTYPE)
    o_ref[...] = _dot(s, w_ref[...].astype(MXU_DTYPE)) + b_ref[...]


def _adaln(cc, w, b):
    rows, d = cc.shape
    n = w.shape[1]
    bn = 1024
    return pl.pallas_call(
        _adaln_kernel,
        grid=(n // bn,),
        in_specs=[pl.BlockSpec((rows, d), lambda j: (0, 0)),
                  pl.BlockSpec((d, bn), lambda j: (0, j)),
                  pl.BlockSpec((1, bn), lambda j: (0, j))],
        out_specs=pl.BlockSpec((rows, bn), lambda j: (0, j)),
        out_shape=jax.ShapeDtypeStruct((rows, n), F32),
        name="adaln",
    )(cc, w, b.reshape(1, n))


def _dwconv(u, w, pos, row_w):
    taps = w.shape[0]
    p = taps // 2
    n = u.shape[0]
    out = u * w[p:p + 1, :]
    for k in range(taps):
        d = k - p
        if d == 0:
            continue
        shifted = pltpu.roll(u, (-d) % n, axis=0)
        valid = jnp.logical_and(pos + d >= 0, pos + d < row_w)
        out = out + jnp.where(valid, shifted, 0.0) * w[k:k + 1, :]
    return out


def _in_proj_kernel(x_ref, sh_ref, sc_ref, gpre_ref, wz_ref, wxbc_ref, wdt_ref, wdtT_ref, wgb_ref, wgc_ref,
                    wv_ref, cw_ref, cb_ref, dtb_ref, dtbT_ref, scw_ref, scg_ref,
                    z_ref, xbc_ref, dt_ref, dtT_ref, sco_ref, *, row_w):
    x = x_ref[0]
    tm = x.shape[0]
    h = _rms(x) * gpre_ref[...]
    h = h * (1.0 + sc_ref[0]) + sh_ref[0]
    hb = h.astype(MXU_DTYPE)
    pos = lax.broadcasted_iota(jnp.int32, (tm, 1), 0) % row_w

    z_ref[0] = _dot(hb, wz_ref[...])
    xbc = _dwconv(_dot(hb, wxbc_ref[...]), cw_ref[...], pos, row_w) + cb_ref[...]
    xbc_ref[0] = _silu(xbc)

    nh = dt_ref.shape[-1]
    dt = _softplus(_dot(hb, wdt_ref[...]) + dtb_ref[...])
    dt_ref[0, 0] = dt[:, :nh]
    dt_ref[1, 0] = dt[:, nh:]
    dtT = _softplus(_dot_nt(wdtT_ref[...], hb) + dtbT_ref[...])
    dtT_ref[0, 0] = dtT[:nh]
    dtT_ref[1, 0] = dtT[nh:]

    gcv = _dot(hb, wgc_ref[...]) * _dot(hb, wv_ref[...])
    sc = _dot(hb, wgb_ref[...]) * _dwconv(gcv, scw_ref[...], pos, row_w)
    sco_ref[0] = _rms(sc) * scg_ref[...]


def _in_proj(x, sh, sc, g_pre, w, conv_w, conv_b, dt_bias, sc_conv_w, sc_norm_g, row_w):
    B, L, D = x.shape
    tm = min(PROJ_TOKENS, L)
    assert L % tm == 0 and tm % row_w == 0
    wz, wxbc, wdt, wgb, wgc, wv = w
    d_ssd, xbc_dim, nh2, d_conv = wz.shape[1], wxbc.shape[1], wdt.shape[1], wgb.shape[1]
    nh = nh2 // 2
    tok = lambda cols: pl.BlockSpec((1, tm, cols), lambda b, i: (b, i, 0))
    per_b = pl.BlockSpec((1, 1, D), lambda b, i: (b, 0, 0))
    return pl.pallas_call(
        functools.partial(_in_proj_kernel, row_w=row_w),
        grid=(B, L // tm),
        in_specs=[tok(D), per_b, per_b, _const_spec((1, D)),
                  _const_spec(wz.shape), _const_spec(wxbc.shape), _const_spec(wdt.shape),
                  _const_spec((nh2, D)), _const_spec(wgb.shape), _const_spec(wgc.shape), _const_spec(wv.shape),
                  _const_spec(conv_w.shape), _const_spec((1, xbc_dim)), _const_spec((1, nh2)),
                  _const_spec((nh2, 1)), _const_spec(sc_conv_w.shape), _const_spec((1, d_conv))],
        out_specs=[tok(d_ssd), tok(xbc_dim),
                   pl.BlockSpec((2, 1, tm, nh), lambda b, i: (0, b, i, 0)),
                   pl.BlockSpec((2, 1, nh, tm), lambda b, i: (0, b, 0, i)),
                   tok(d_conv)],
        out_shape=[jax.ShapeDtypeStruct((B, L, d_ssd), F32),
                   jax.ShapeDtypeStruct((B, L, xbc_dim), F32),
                   jax.ShapeDtypeStruct((2, B, L, nh), F32),
                   jax.ShapeDtypeStruct((2, B, nh, L), F32),
                   jax.ShapeDtypeStruct((B, L, d_conv), F32)],
        compiler_params=pltpu.CompilerParams(dimension_semantics=("parallel", "parallel"),
                                             vmem_limit_bytes=VMEM_LIMIT_BYTES),
        name="in_proj",
    )(x, sh, sc, g_pre.reshape(1, D), wz, wxbc, wdt, wdt.T, wgb, wgc, wv,
      conv_w, conv_b.reshape(1, xbc_dim), dt_bias.reshape(1, nh2), dt_bias.reshape(nh2, 1),
      sc_conv_w, sc_norm_g.reshape(1, d_conv))


def _ssd_kernel(xs_ref, bm_ref, cm_ref, dt_ref, dtT_ref, alog_ref, alogT_ref, h0_ref, y_ref, hf_ref, st_ref):
    d = pl.program_id(1)
    c = pl.program_id(2)
    nc = pl.num_programs(2)
    Q = SSD_CHUNK
    N = SSD_STATE
    P = SSD_HEAD_DIM
    d_ssd = xs_ref.shape[-1]
    nh = dt_ref.shape[-1]
    hpg = nh // SSD_GROUPS
    gw = hpg * P

    @pl.when(c == 0)
    def _():
        st_ref[...] = h0_ref[0, 0]

    a = dt_ref[0, 0] * (-jnp.exp(alog_ref[0]))
    aT = dtT_ref[0, 0] * (-jnp.exp(alogT_ref[0]))
    row = lax.broadcasted_iota(jnp.int32, (Q, Q), 0)
    col = lax.broadcasted_iota(jnp.int32, (Q, Q), 1)
    fwd = d == 0
    mask = (row - col) * (1 - 2 * d) >= 0
    maskb = mask.astype(F32).astype(MXU_DTYPE)
    acs = sum(_dot(maskb, p) for p in _split_bf16(a, 3))
    acsT = sum(_dot_nt(p, maskb) for p in _split_bf16(aT, 3))

    erow = lax.broadcasted_iota(jnp.int32, (nh, d_ssd), 0)
    ecol = lax.broadcasted_iota(jnp.int32, (nh, d_ssd), 1)
    expand = (ecol // P == erow).astype(F32).astype(MXU_DTYPE)
    acs_e = sum(_dot(p, expand) for p in _split_bf16(acs, 2))
    dt_e = sum(_dot(p, expand) for p in _split_bf16(dt_ref[0, 0], 2))
    tot_e = jnp.where(fwd, acs_e[Q - 1:Q, :], acs_e[0:1, :])

    xdt = xs_ref[0] * dt_e
    xw = (xdt * jnp.exp(tot_e - acs_e)).astype(MXU_DTYPE)
    xdtb = xdt.astype(MXU_DTYPE)
    eacs = jnp.exp(acs_e)
    cdec = jnp.exp(tot_e)
    lane = lax.broadcasted_iota(jnp.int32, (Q, 2 * P), 1)

    for g in range(SSD_GROUPS):
        bm = bm_ref[0, :, g * N:(g + 1) * N].astype(MXU_DTYPE)
        cm = cm_ref[0, :, g * N:(g + 1) * N].astype(MXU_DTYPE)
        cb = _dot_nt(cm, bm)
        hT = st_ref[g * N:(g + 1) * N, :]
        gcols = slice(g * gw, (g + 1) * gw)
        y_off = _dot(cm, hT.astype(MXU_DTYPE)) * eacs[:, gcols]
        st_ref[g * N:(g + 1) * N, :] = hT * cdec[:, gcols] + _dot_tn(bm, xw[:, gcols])
        for pr in range(hpg // 2):
            pcols = slice(g * gw + pr * 2 * P, g * gw + (pr + 1) * 2 * P)
            xpair = xdtb[:, pcols]
            halves = []
            for r in range(2):
                hd = g * hpg + pr * 2 + r
                seg = acs[:, hd:hd + 1] - acsT[hd:hd + 1, :]
                decay = jnp.exp(jnp.where(mask, seg, -jnp.inf))
                halves.append(_dot((cb * decay).astype(MXU_DTYPE), xpair))
            y_diag = jnp.where(lane < P, halves[0], halves[1])
            y_ref[0, 0, :, pcols] = y_diag + y_off[:, pr * 2 * P:(pr + 1) * 2 * P]

    @pl.when(c == nc - 1)
    def _():
        hf_ref[0, 0] = st_ref[...]


def _ssd_scan(xbc, dt, dtT, a_log, h0):
    B, L, _ = xbc.shape
    nh = dt.shape[-1]
    d_ssd = nh * SSD_HEAD_DIM
    gn = SSD_GROUPS * SSD_STATE
    Q = SSD_CHUNK
    nc = L // Q
    gw = d_ssd // SSD_GROUPS
    chunk = lambda d, c: c + d * (nc - 1 - 2 * c)
    return pl.pallas_call(
        _ssd_kernel,
        grid=(B, 2, nc),
        in_specs=[pl.BlockSpec((1, Q, d_ssd), lambda b, d, c: (b, chunk(d, c), 0)),
                  pl.BlockSpec((1, Q, gn), lambda b, d, c: (b, chunk(d, c), d_ssd // gn)),
                  pl.BlockSpec((1, Q, gn), lambda b, d, c: (b, chunk(d, c), d_ssd // gn + 1)),
                  pl.BlockSpec((1, 1, Q, nh), lambda b, d, c: (d, b, chunk(d, c), 0)),
                  pl.BlockSpec((1, 1, nh, Q), lambda b, d, c: (d, b, 0, chunk(d, c))),
                  pl.BlockSpec((1, 1, nh), lambda b, d, c: (d, 0, 0)),
                  pl.BlockSpec((1, nh, 1), lambda b, d, c: (d, 0, 0)),
                  pl.BlockSpec((1, 1, gn, gw), lambda b, d, c: (b, d, 0, 0))],
        out_specs=[pl.BlockSpec((1, 1, Q, d_ssd), lambda b, d, c: (d, b, chunk(d, c), 0)),
                   pl.BlockSpec((1, 1, gn, gw), lambda b, d, c: (b, d, 0, 0))],
        out_shape=[jax.ShapeDtypeStruct((2, B, L, d_ssd), F32),
                   jax.ShapeDtypeStruct((B, 2, gn, gw), F32)],
        scratch_shapes=[pltpu.VMEM((gn, gw), F32)],
        compiler_params=pltpu.CompilerParams(dimension_semantics=("parallel", "arbitrary", "arbitrary"),
                                             vmem_limit_bytes=VMEM_LIMIT_BYTES),
        name="ssd_scan",
    )(xbc, xbc, xbc, dt, dtT, a_log.reshape(2, 1, nh), a_log.reshape(2, nh, 1), h0)


def _mix_out_kernel(x_ref, yf_ref, yb_ref, xs_ref, z_ref, sco_ref, dsk_ref, ng_ref, wo1_ref, wo2_ref,
                    gpost_ref, gt_ref, gpre_ref, sh_ref, sc_ref, x1_ref, h2_ref):
    d_ssd = xs_ref.shape[-1]
    gw = d_ssd // SSD_GROUPS
    y = yf_ref[0, 0] + yb_ref[0, 0] + dsk_ref[...] * xs_ref[0]
    y = y * _silu(z_ref[0])
    parts = [_rms(y[:, g * gw:(g + 1) * gw]) for g in range(SSD_GROUPS)]
    yn = jnp.concatenate(parts, axis=-1) * ng_ref[...]
    mix = _dot(yn.astype(MXU_DTYPE), wo1_ref[...]) + _dot(sco_ref[0].astype(MXU_DTYPE), wo2_ref[...])
    x1 = x_ref[0] + gt_ref[0] * (_rms(mix) * gpost_ref[...])
    x1_ref[0] = x1
    h2 = _rms(x1) * gpre_ref[...]
    h2_ref[0] = h2 * (1.0 + sc_ref[0]) + sh_ref[0]


def _mix_out(x, y2, xbc, z, sco, d_skip_e, norm_g, wo1, wo2, g_post, gt1, g_pre_ffn, sh2, sc2):
    B, L, D = x.shape
    d_ssd = z.shape[-1]
    d_conv = sco.shape[-1]
    tm = min(PROJ_TOKENS, L)
    tok = lambda cols: pl.BlockSpec((1, tm, cols), lambda b, i: (b, i, 0))
    per_b = pl.BlockSpec((1, 1, D), lambda b, i: (b, 0, 0))
    ydir = lambda d: pl.BlockSpec((1, 1, tm, d_ssd), lambda b, i: (d, b, i, 0))
    return pl.pallas_call(
        _mix_out_kernel,
        grid=(B, L // tm),
        in_specs=[tok(D), ydir(0), ydir(1), tok(d_ssd), tok(d_ssd), tok(d_conv),
                  _const_spec((1, d_ssd)), _const_spec((1, d_ssd)), _const_spec(wo1.shape), _const_spec(wo2.shape),
                  _const_spec((1, D)), per_b, _const_spec((1, D)), per_b, per_b],
        out_specs=[tok(D), tok(D)],
        out_shape=[jax.ShapeDtypeStruct((B, L, D), F32), jax.ShapeDtypeStruct((B, L, D), F32)],
        compiler_params=pltpu.CompilerParams(dimension_semantics=("parallel", "parallel"),
                                             vmem_limit_bytes=VMEM_LIMIT_BYTES),
        name="mix_out",
    )(x, y2, y2, xbc, z, sco, d_skip_e.reshape(1, d_ssd), norm_g.reshape(1, d_ssd), wo1, wo2,
      g_post.reshape(1, D), gt1, g_pre_ffn.reshape(1, D), sh2, sc2)


def _topk_rows(s, k):
    n = s.shape[0]
    rows = lax.broadcasted_iota(jnp.int32, s.shape, 0).astype(F32)
    vals, idxs = [], []
    for _ in range(k):
        m = jnp.max(s, axis=0, keepdims=True)
        i = jnp.min(jnp.where(s == m, rows, float(n)), axis=0, keepdims=True)
        vals.append(m)
        idxs.append(i)
        s = jnp.where(rows == i, -jnp.inf, s)
    return jnp.concatenate(vals, axis=0), jnp.concatenate(idxs, axis=0)


def _pair_candidates(k):
    return [(i, j) for i in range(k) for j in range(k) if (i + 1) * (j + 1) <= k]


def _pair_select_matrices(k):
    pairs = _pair_candidates(k)
    rows = -(-len(pairs) // 8) * 8
    sel0 = np.zeros((rows, k), np.float32)
    sel1 = np.zeros((rows, k), np.float32)
    bias = np.full((rows, 1), -np.inf, np.float32)
    for r, (i, j) in enumerate(pairs):
        sel0[r, i] = 1.0
        sel1[r, j] = 1.0
        bias[r, 0] = 0.0
    return sel0, sel1, bias


def _select_rows(sel, x):
    p = _split_bf16(x, 3)
    return (_dot(sel, p[0]) + _dot(sel, p[1])) + _dot(sel, p[2])


def _route_kernel(h2_ref, wq_ref, keys_ref, sel0_ref, sel1_ref, bias_ref, eid_ref, g_ref, eid_s, g_s):
    K = PEER_TOPK
    NK = PEER_N_KEYS
    hb = h2_ref[...].astype(MXU_DTYPE)
    T = hb.shape[0]
    half = keys_ref.shape[-1]
    sel0 = sel0_ref[...]
    sel1 = sel1_ref[...]
    n_cand = sel0.shape[0]
    c_rows = lax.broadcasted_iota(jnp.int32, (n_cand, T), 0).astype(F32)

    def head(h, carry):
        q = _dot(hb, wq_ref[h]).astype(MXU_DTYPE)
        sv, si = [], []
        for p in range(2):
            s = _dot_nt(keys_ref[h, p], q[:, p * half:(p + 1) * half])
            v, i = _topk_rows(s, K)
            sv.append(v)
            si.append(i)
        cand = _select_rows(sel0, sv[0]) + _select_rows(sel1, sv[1]) + bias_ref[...]
        ecand = _dot(sel0, si[0].astype(MXU_DTYPE)) * float(NK) + _dot(sel1, si[1].astype(MXU_DTYPE))
        cv, eid = [], []
        for _ in range(K):
            m = jnp.max(cand, axis=0, keepdims=True)
            i = jnp.min(jnp.where(cand == m, c_rows, float(n_cand)), axis=0, keepdims=True)
            hit = c_rows == i
            cv.append(m)
            eid.append(jnp.max(jnp.where(hit, ecand, -1.0), axis=0, keepdims=True))
            cand = jnp.where(hit, -jnp.inf, cand)
        cvs = jnp.concatenate(cv, axis=0)
        ex = jnp.exp(cvs - cv[0])
        gate = ex / jnp.sum(ex, axis=0, keepdims=True)
        rows = pl.ds(pl.multiple_of(h * K, K), K)
        eid_s[rows, :] = jnp.concatenate(eid, axis=0)
        g_s[rows, :] = gate
        return carry

    lax.fori_loop(0, PEER_HEADS, head, 0)
    eid_ref[...] = eid_s[...].T.astype(jnp.int32)
    g_ref[...] = g_s[...].T


def _peer_route(h2, wq, keys):
    ntok, D = h2.shape
    T = ROUTE_TOKENS
    slots = PEER_HEADS * PEER_TOPK
    sel0, sel1, bias = _pair_select_matrices(PEER_TOPK)
    sel0 = jnp.asarray(sel0, MXU_DTYPE)
    sel1 = jnp.asarray(sel1, MXU_DTYPE)
    return pl.pallas_call(
        _route_kernel,
        grid=(ntok // T,),
        in_specs=[pl.BlockSpec((T, D), lambda i: (i, 0)), _const_spec(wq.shape), _const_spec(keys.shape),
                  _const_spec(sel0.shape), _const_spec(sel1.shape), _const_spec(bias.shape)],
        out_specs=[pl.BlockSpec((T, slots), lambda i: (i, 0)), pl.BlockSpec((T, slots), lambda i: (i, 0))],
        out_shape=[jax.ShapeDtypeStruct((ntok, slots), jnp.int32), jax.ShapeDtypeStruct((ntok, slots), F32)],
        scratch_shapes=[pltpu.VMEM((slots, T), F32), pltpu.VMEM((slots, T), F32)],
        compiler_params=pltpu.CompilerParams(dimension_semantics=("parallel",),
                                             vmem_limit_bytes=VMEM_LIMIT_BYTES),
        name="peer_route",
    )(h2, wq, keys, sel0, sel1, jnp.asarray(bias))


def _expert_eval(chunk, x_row, g_row, eye):
    D = x_row.shape[-1]
    C = D // LANES
    g_col = jnp.sum(jnp.where(eye, g_row, 0.0), axis=-1, keepdims=True)
    acc = chunk(0) * x_row[:, 0:LANES]
    for c in range(1, C):
        acc = acc + chunk(c) * x_row[:, c * LANES:(c + 1) * LANES]
    w = g_col * _gelu_tanh(jnp.sum(acc, axis=-1, keepdims=True))
    outs = [jnp.sum(w * chunk(C + c), axis=0, keepdims=True) for c in range(C)]
    return jnp.concatenate(outs, axis=-1)


def _peer_kernel(eid_hbm, tab_hbm, x_ref, g_ref, x1_ref, gt_ref, gpost_ref, o_ref, *scratch):
    R = PEER_RING
    e_bufs, bufs, rsem, esem, peer_s = scratch[:2], scratch[2:2 + R], scratch[2 + R], scratch[3 + R], scratch[4 + R]
    i = pl.program_id(0)
    n = pl.num_programs(0)
    T, D = x_ref.shape
    S = g_ref.shape[-1]
    C = D // LANES
    G = T // R
    gsz = R * S

    def eid_copy(grp, par):
        return pltpu.make_async_copy(eid_hbm.at[pl.ds(grp * gsz, gsz)], e_bufs[par], esem.at[par])

    def issue(slot, par, tok):
        for k in range(S):
            e = e_bufs[par][tok * S + k]
            pltpu.async_copy(tab_hbm.at[pl.ds(e * (2 * C), 2 * C), :], bufs[slot].at[k], rsem.at[slot],
                             priority=k % 2)

    def wait(slot):
        pltpu.make_async_copy(bufs[slot], bufs[slot], rsem.at[slot]).wait()

    @pl.when(i == 0)
    def _():
        eid_copy(0, 0).start()
        eid_copy(1, 1).start()
        eid_copy(0, 0).wait()
        for t in range(R - 1):
            issue(t, 0, t)

    eye = lax.broadcasted_iota(jnp.int32, (S, S), 0) == lax.broadcasted_iota(jnp.int32, (S, S), 1)

    def evaluate(slot, t):
        buf = bufs[slot]
        groups = [jnp.swapaxes(buf[pl.ds(8 * m, 8)], 0, 1) for m in range(S // 8)]
        chunk = lambda c: jnp.concatenate([gm[c] for gm in groups], axis=0)
        peer_s[pl.ds(t, 1), :] = _expert_eval(chunk, x_ref[pl.ds(t, 1), :], g_ref[pl.ds(t, 1), :], eye)

    def group(grp, par, t0):
        for j in range(R):
            wait(j)
            if j == 0:
                issue(R - 1, par, R - 1)
                eid_copy(grp + 2, par).start()
            else:
                if j == 1:
                    eid_copy(grp + 1, 1 - par).wait()
                issue(j - 1, 1 - par, j - 1)
            evaluate(j, t0 + j)

    def body(gp, carry):
        for par in range(2):
            gi = 2 * gp + par
            group(i * G + gi, par, pl.multiple_of(gi * R, R))
        return carry

    lax.fori_loop(0, G // 2, body, 0)
    o_ref[...] = x1_ref[...] + gt_ref[0] * (_rms(peer_s[...]) * gpost_ref[...])

    @pl.when(i == n - 1)
    def _():
        for j in range(R - 1):
            wait(j)
        eid_copy(n * G + 1, 1).wait()


def _peer_experts(eid_flat, g, h2, tab, x1, gt2, g_post):
    ntok, D = h2.shape
    S = g.shape[-1]
    T = PEER_TOK_BLOCK
    C = D // LANES
    steps_per_batch = ntok // gt2.shape[0] // T
    assert T % (2 * PEER_RING) == 0 and ntok == steps_per_batch * T * gt2.shape[0]
    return pl.pallas_call(
        _peer_kernel,
        grid=(ntok // T,),
        in_specs=[pl.BlockSpec(memory_space=pl.ANY),
                  pl.BlockSpec(memory_space=pl.ANY),
                  pl.BlockSpec((T, D), lambda i: (i, 0)),
                  pl.BlockSpec((T, S), lambda i: (i, 0)),
                  pl.BlockSpec((T, D), lambda i: (i, 0)),
                  pl.BlockSpec((1, 1, D), lambda i: (i // steps_per_batch, 0, 0)),
                  pl.BlockSpec((1, D), lambda i: (0, 0))],
        out_specs=pl.BlockSpec((T, D), lambda i: (i, 0)),
        out_shape=jax.ShapeDtypeStruct((ntok, D), F32),
        scratch_shapes=[pltpu.SMEM((PEER_RING * S,), jnp.int32) for _ in range(2)]
        + [pltpu.VMEM((S, 2 * C, LANES), F32) for _ in range(PEER_RING)]
        + [pltpu.SemaphoreType.DMA((PEER_RING,)), pltpu.SemaphoreType.DMA((2,)), pltpu.VMEM((T, D), F32)],
        compiler_params=pltpu.CompilerParams(dimension_semantics=("arbitrary",),
                                             vmem_limit_bytes=VMEM_LIMIT_BYTES),
        name="peer_experts",
    )(eid_flat, tab, h2, g, x1, gt2, g_post.reshape(1, D))


def _split_w_in(w_in, d_ssd, xbc_dim, nh2, d_conv):
    bounds = [0, d_ssd, d_ssd + xbc_dim, d_ssd + xbc_dim + nh2]
    bounds += [bounds[-1] + d_conv, bounds[-1] + 2 * d_conv, bounds[-1] + 3 * d_conv]
    wz, wxbc, wdt, wgb, wgc, wv = (w_in[:, bounds[i]:bounds[i + 1]].astype(MXU_DTYPE) for i in range(6))
    return wz, wxbc, wdt, wgb, wgc, wv


def kernel(x, c, ctx, c_ctx, w_ada, b_ada, g_pre_mix, g_post_mix, g_pre_ffn, g_post_ffn, w_in, ssd_conv_w, ssd_conv_b, ssd_dt_bias, ssd_a_log, ssd_d, ssd_norm_g, sc_conv_w, sc_norm_g, w_out, peer_w_q, peer_keys, peer_u, peer_v):
    B, L, D = x.shape
    ctx_len = ctx.shape[1]
    assert w_ada.shape[0] == 1, "single-layer block"
    nh2 = ssd_dt_bias.shape[1] * ssd_dt_bias.shape[2]
    nh = nh2 // 2
    d_ssd = nh * SSD_HEAD_DIM
    xbc_dim = ssd_conv_w.shape[-1]
    d_conv = sc_conv_w.shape[-1]
    n_mod = w_ada.shape[-1] // D

    pad_rows = (-(B + 1)) % 8
    cc = jnp.concatenate([c, c_ctx[None, :], jnp.zeros((pad_rows, D), F32)], axis=0)
    mod = _adaln(cc, w_ada[0], b_ada[0])
    mod_l = mod[:B].reshape(B, n_mod, 1, D)
    mod_c = jnp.broadcast_to(mod[B].reshape(1, n_mod, 1, D), (B, n_mod, 1, D))
    sh1_l, sc1_l, gt1_l, sh2_l, sc2_l, gt2_l = (mod_l[:, j] for j in range(n_mod))
    sh1_c, sc1_c = mod_c[:, 0], mod_c[:, 1]

    w = _split_w_in(w_in[0], d_ssd, xbc_dim, nh2, d_conv)
    proj_args = (w, ssd_conv_w[0], ssd_conv_b[0], ssd_dt_bias[0].reshape(nh2), sc_conv_w[0], sc_norm_g[0])
    a_log = ssd_a_log[0]

    _, xbc_c, dt_c, dtT_c, _ = _in_proj(ctx, sh1_c, sc1_c, g_pre_mix[0], *proj_args, row_w=ctx_len)
    gn = SSD_GROUPS * SSD_STATE
    h0 = jnp.zeros((B, 2, gn, d_ssd // SSD_GROUPS), F32)
    _, h_ctx = _ssd_scan(xbc_c, dt_c, dtT_c, a_log, h0)

    z_l, xbc_l, dt_l, dtT_l, sco_l = _in_proj(x, sh1_l, sc1_l, g_pre_mix[0], *proj_args, row_w=GRID_W)
    y2, _ = _ssd_scan(xbc_l, dt_l, dtT_l, a_log, h_ctx)

    wo = w_out[0].astype(MXU_DTYPE)
    d_skip_e = jnp.repeat(ssd_d[0], SSD_HEAD_DIM)
    x1, h2 = _mix_out(x, y2, xbc_l, z_l, sco_l, d_skip_e, ssd_norm_g[0], wo[:d_ssd], wo[d_ssd:],
                      g_post_mix[0], gt1_l, g_pre_ffn[0], sh2_l, sc2_l)

    ntok = B * L
    qd = peer_w_q.shape[-1] // PEER_HEADS
    wq = peer_w_q[0].reshape(D, PEER_HEADS, qd).transpose(1, 0, 2).astype(MXU_DTYPE)
    eid_tok, g_tok = _peer_route(h2.reshape(ntok, D), wq, peer_keys[0].astype(MXU_DTYPE))
    n_exp = peer_u.shape[1]
    tab = jnp.concatenate([peer_u[0].reshape(n_exp, -1, LANES), peer_v[0].reshape(n_exp, -1, LANES)],
                          axis=1).reshape(-1, LANES)
    eid = jnp.pad(eid_tok.reshape(-1), (0, 2 * PEER_RING * eid_tok.shape[1]))
    out = _peer_experts(eid, g_tok, h2.reshape(ntok, D), tab, x1.reshape(ntok, D), gt2_l, g_post_ffn[0])
    return out.reshape(B, L, D)
```
